```python
import math
import jax
import jax.numpy as jnp
from jax import lax
import numpy as np

D_MODEL = 1024
BATCH = 32
SEQ = 256
DEPTH = 4
DEC_BATCH = 2
DEC_SEQ = 2048
PAST_LEN = 512

GRID_W = 64
ROPE_THETA = 10000.0
Q_BLOCK = 128
EPS = 1e-6
N_BRANCH = 4
BRANCH_W = D_MODEL // 4
FNET_GROUPS = 4
FNET_GW = BRANCH_W // FNET_GROUPS
GQA_HEADS = 4
GQA_KV_HEADS = 2
GQA_HEAD_DIM = BRANCH_W // GQA_HEADS
DIFF_HEADS = 4
DIFF_V_DIM = BRANCH_W // DIFF_HEADS
DIFF_QK_DIM = DIFF_V_DIM // 2
SGU_GROUPS = 4
SGU_CHUNK = 128
SGU_GW = BRANCH_W // SGU_GROUPS
D_FF = 4 * D_MODEL
N_MOD = 6

_IN_SPLITS = (
    BRANCH_W,
    GQA_HEADS * GQA_HEAD_DIM,
    GQA_KV_HEADS * GQA_HEAD_DIM,
    GQA_KV_HEADS * GQA_HEAD_DIM,
    DIFF_HEADS * 2 * DIFF_QK_DIM,
    DIFF_HEADS * 2 * DIFF_QK_DIM,
    DIFF_HEADS * DIFF_V_DIM,
    BRANCH_W,
    BRANCH_W,
)
IN_W = sum(_IN_SPLITS)

kernel_name = "hybrid_diffusion_prefix_trunk_step"


def _rms_norm(x, g):
    xf = x.astype(jnp.float32)
    y = xf * lax.rsqrt(jnp.mean(xf * xf, axis=-1, keepdims=True) + EPS)
    return (y * g.astype(jnp.float32)).astype(x.dtype)


def _split_cols(proj):
    idx = []
    acc = 0
    for s in _IN_SPLITS[:-1]:
        acc += s
        idx.append(acc)
    return jnp.split(proj, idx, axis=-1)


def _axial_rope(length, dim):
    rows = length // GRID_W
    row = jnp.repeat(jnp.arange(rows, dtype=jnp.float32), GRID_W)
    col = jnp.tile(jnp.arange(GRID_W, dtype=jnp.float32), rows)
    quarter = dim // 4
    inv = ROPE_THETA ** (-jnp.arange(quarter, dtype=jnp.float32) / quarter)
    ang = jnp.concatenate([row[:, None] * inv, col[:, None] * inv], axis=-1)
    return jnp.cos(ang), jnp.sin(ang)


def _apply_rope(x, cos, sin):
    half = x.shape[-1] // 2
    shape = (1, x.shape[1]) + (1,) * (x.ndim - 3) + (half,)
    c = cos.reshape(shape)
    s = sin.reshape(shape)
    xf = x.astype(jnp.float32)
    x1, x2 = xf[..., :half], xf[..., half:]
    return jnp.concatenate([x1 * c - x2 * s, x1 * s + x2 * c], axis=-1).astype(x.dtype)


def _gqa_attention(q, k, v):
    b, sq, h, dh = q.shape
    hkv = k.shape[2]
    g = h // hkv
    nb = sq // Q_BLOCK
    scale = 1.0 / math.sqrt(dh)
    qb = q.reshape(b, nb, Q_BLOCK, hkv, g, dh).transpose(1, 0, 2, 3, 4, 5)

    def one_block(qblk):
        s = jnp.einsum("bqkgd,bskd->bkgqs", qblk, k, preferred_element_type=jnp.float32) * scale
        p = jax.nn.softmax(s, axis=-1).astype(v.dtype)
        return jnp.einsum("bkgqs,bskd->bqkgd", p, v)

    o = lax.map(one_block, qb)
    return o.transpose(1, 0, 2, 3, 4, 5).reshape(b, sq, h * dh)


def _diff_attention(q, k, v, lam):
    b, sq, h, _, dq = q.shape
    nb = sq // Q_BLOCK
    scale = 1.0 / math.sqrt(dq)
    qb = q.reshape(b, nb, Q_BLOCK, h, 2, dq).transpose(1, 0, 2, 3, 4, 5)

    def one_block(qblk):
        s = jnp.einsum("bqhmd,bshmd->bhmqs", qblk, k, preferred_element_type=jnp.float32) * scale
        p = jax.nn.softmax(s, axis=-1)
        w = (p[:, :, 0] - lam * p[:, :, 1]).astype(v.dtype)
        return jnp.einsum("bhqs,bshd->bqhd", w, v)

    o = lax.map(one_block, qb)
    return o.transpose(1, 0, 2, 3, 4).reshape(b, sq, h, v.shape[-1])


def _mixer(h, lp, lam, lam_scale, ctx):
    b, length, _ = h.shape
    a, bq, bk, bv, cq, ck, cv, du, dv = _split_cols(h @ lp["w_in"])

    af = a.reshape(b, length, FNET_GROUPS, FNET_GW).astype(jnp.float32)
    af = jnp.fft.fft2(af, axes=(1, 3), norm="ortho").real
    out_a = af.astype(h.dtype).reshape(b, length, BRANCH_W) @ lp["w_fourier"]

    bq = _rms_norm(bq.reshape(b, length, GQA_HEADS, GQA_HEAD_DIM), lp["q_norm_g"])
    bk = _rms_norm(bk.reshape(b, length, GQA_KV_HEADS, GQA_HEAD_DIM), lp["k_norm_g"])
    bv = bv.reshape(b, length, GQA_KV_HEADS, GQA_HEAD_DIM)
    cq = cq.reshape(b, length, DIFF_HEADS, 2, DIFF_QK_DIM)
    ck = ck.reshape(b, length, DIFF_HEADS, 2, DIFF_QK_DIM)
    cv = cv.reshape(b, length, DIFF_HEADS, DIFF_V_DIM)

    if ctx is None:
        new_kv = (bk, bv, ck, cv)
        kb, vb, kc, vc = bk, bv, ck, cv
    else:
        new_kv = None
        cos_b, sin_b = _axial_rope(length, GQA_HEAD_DIM)
        cos_c, sin_c = _axial_rope(length, DIFF_QK_DIM)
        bq = _apply_rope(bq, cos_b, sin_b)
        bk = _apply_rope(bk, cos_b, sin_b)
        cq = _apply_rope(cq, cos_c, sin_c)
        ck = _apply_rope(ck, cos_c, sin_c)
        ctx_bk, ctx_bv, ctx_ck, ctx_cv = ctx
        kb = jnp.concatenate([ctx_bk, bk], axis=1)
        vb = jnp.concatenate([ctx_bv, bv], axis=1)
        kc = jnp.concatenate([ctx_ck, ck], axis=1)
        vc = jnp.concatenate([ctx_cv, cv], axis=1)

    out_b = _gqa_attention(bq, kb, vb)
    oc = _diff_attention(cq, kc, vc, lam)
    out_c = (_rms_norm(oc, lp["diff_norm_g"]) * lam_scale).reshape(b, length, BRANCH_W)

    du = jax.nn.gelu(du)
    dv = _rms_norm(jax.nn.gelu(dv), lp["sgu_norm_g"])
    dv = dv.reshape(b, length // SGU_CHUNK, SGU_CHUNK, SGU_GROUPS, SGU_GW)
    s = jnp.einsum("gqp,bnpgc->bnqgc", lp["w_spatial"], dv) + lp["b_spatial"].T[None, None, :, :, None]
    out_d = du * s.reshape(b, length, BRANCH_W)

    branches = jnp.stack([out_a, out_b, out_c, out_d], axis=2)
    bproj = jnp.einsum("blnc,ncd->blnd", branches, lp["w_branch"])
    gates = jax.nn.sigmoid(h @ lp["w_gate"]).reshape(b, length, N_BRANCH, D_MODEL)
    merged = jnp.sum(gates * bproj, axis=2)
    return merged @ lp["w_out"], new_kv


def _block(x, cond, lp, lam, lam_scale, ctx):
    mod = (jax.nn.silu(cond) @ lp["w_ada"] + lp["b_ada"])[:, None, :]
    sh1, sc1, g1, sh2, sc2, g2 = jnp.split(mod, N_MOD, axis=-1)
    h = _rms_norm(x, lp["norm1_g"]) * (1 + sc1) + sh1
    m, new_kv = _mixer(h, lp, lam, lam_scale, ctx)
    x = x + g1 * m
    h = _rms_norm(x, lp["norm2_g"]) * (1 + sc2) + sh2
    f = jnp.square(jax.nn.relu(h @ lp["w_mlp1"])) @ lp["w_mlp2"]
    return x + g2 * f, new_kv


def setup_inputs(seed: int = 0) -> dict:
    key = jax.random.key(seed)
    ks = jax.random.split(key, 32)
    f32 = jnp.float32

    def nrm(k, shape, scale):
        return jax.random.normal(k, shape, f32) * scale

    def gain(k, shape):
        return 1.0 + 0.01 * jax.random.normal(k, shape, f32)

    return {
        "x_prompt": nrm(ks[0], (BATCH, SEQ, D_MODEL), 1.0),
        "x_sample": nrm(ks[1], (DEC_BATCH, DEC_SEQ, D_MODEL), 1.0),
        "c": nrm(ks[2], (DEC_BATCH, D_MODEL), 1.0),
        "cache_gqa_k": nrm(ks[3], (DEC_BATCH, DEPTH, PAST_LEN, GQA_KV_HEADS, GQA_HEAD_DIM), 1.0),
        "cache_gqa_v": nrm(ks[4], (DEC_BATCH, DEPTH, PAST_LEN, GQA_KV_HEADS, GQA_HEAD_DIM), 1.0),
        "cache_diff_k": nrm(ks[5], (DEC_BATCH, DEPTH, PAST_LEN, DIFF_HEADS, 2, DIFF_QK_DIM), 1.0),
        "cache_diff_v": nrm(ks[6], (DEC_BATCH, DEPTH, PAST_LEN, DIFF_HEADS, DIFF_V_DIM), 1.0),
        "c_ctx": nrm(ks[7], (D_MODEL,), 1.0),
        "w_ada": nrm(ks[8], (DEPTH, D_MODEL, N_MOD * D_MODEL), 0.5 * D_MODEL ** -0.5),
        "b_ada": nrm(ks[9], (DEPTH, N_MOD * D_MODEL), 0.01),
        "norm1_g": gain(ks[10], (DEPTH, D_MODEL)),
        "norm2_g": gain(ks[11], (DEPTH, D_MODEL)),
        "w_in": nrm(ks[12], (DEPTH, D_MODEL, IN_W), D_MODEL ** -0.5),
        "w_fourier": nrm(ks[13], (DEPTH, BRANCH_W, BRANCH_W), BRANCH_W ** -0.5),
        "q_norm_g": gain(ks[14], (DEPTH, GQA_HEAD_DIM)),
        "k_norm_g": gain(ks[15], (DEPTH, GQA_HEAD_DIM)),
        "lambda_q1": nrm(ks[16], (DEPTH, DIFF_QK_DIM), 0.1),
        "lambda_k1": nrm(ks[17], (DEPTH, DIFF_QK_DIM), 0.1),
        "lambda_q2": nrm(ks[18], (DEPTH, DIFF_QK_DIM), 0.1),
        "lambda_k2": nrm(ks[19], (DEPTH, DIFF_QK_DIM), 0.1),
        "diff_norm_g": gain(ks[20], (DEPTH, DIFF_V_DIM)),
        "sgu_norm_g": gain(ks[21], (DEPTH, BRANCH_W)),
        "w_spatial": nrm(ks[22], (DEPTH, SGU_GROUPS, SGU_CHUNK, SGU_CHUNK), SGU_CHUNK ** -0.5),
        "b_spatial": nrm(ks[23], (DEPTH, SGU_GROUPS, SGU_CHUNK), 0.01),
        "w_gate": nrm(ks[24], (DEPTH, D_MODEL, N_BRANCH * D_MODEL), D_MODEL ** -0.5),
        "w_branch": nrm(ks[25], (DEPTH, N_BRANCH, BRANCH_W, D_MODEL), BRANCH_W ** -0.5),
        "w_out": nrm(ks[26], (DEPTH, D_MODEL, D_MODEL), D_MODEL ** -0.5),
        "w_mlp1": nrm(ks[27], (DEPTH, D_MODEL, D_FF), D_MODEL ** -0.5),
        "w_mlp2": nrm(ks[28], (DEPTH, D_FF, D_MODEL), D_FF ** -0.5),
        "final_norm_g": gain(ks[29], (D_MODEL,)),
    }


def reference(x_prompt, x_sample, c, cache_gqa_k, cache_gqa_v, cache_diff_k, cache_diff_v,
              c_ctx, w_ada, b_ada, norm1_g, norm2_g, w_in, w_fourier, q_norm_g, k_norm_g,
              lambda_q1, lambda_k1, lambda_q2, lambda_k2, diff_norm_g, sgu_norm_g,
              w_spatial, b_spatial, w_gate, w_branch, w_out, w_mlp1, w_mlp2, final_norm_g):
    xp = x_prompt
    xs = x_sample
    cond_ctx = c_ctx[None, :]
    gk, gv, dk, dv = [], [], [], []
    for l in range(DEPTH):
        lp = {
            "w_ada": w_ada[l], "b_ada": b_ada[l], "norm1_g": norm1_g[l], "norm2_g": norm2_g[l],
            "w_in": w_in[l], "w_fourier": w_fourier[l], "q_norm_g": q_norm_g[l],
            "k_norm_g": k_norm_g[l], "diff_norm_g": diff_norm_g[l], "sgu_norm_g": sgu_norm_g[l],
            "w_spatial": w_spatial[l], "b_spatial": b_spatial[l], "w_gate": w_gate[l],
            "w_branch": w_branch[l], "w_out": w_out[l], "w_mlp1": w_mlp1[l], "w_mlp2": w_mlp2[l],
        }
        lam_init = 0.8 - 0.6 * math.exp(-0.3 * l)
        lam = (jnp.exp(jnp.sum(lambda_q1[l].astype(jnp.float32) * lambda_k1[l].astype(jnp.float32)))
               - jnp.exp(jnp.sum(lambda_q2[l].astype(jnp.float32) * lambda_k2[l].astype(jnp.float32)))
               + lam_init)
        lam_scale = 1.0 - lam_init
        xp, (bk, bv, ck, cv) = _block(xp, cond_ctx, lp, lam, lam_scale, None)
        gk.append(bk)
        gv.append(bv)
        dk.append(ck)
        dv.append(cv)
        ctx = (cache_gqa_k[:, l], cache_gqa_v[:, l], cache_diff_k[:, l], cache_diff_v[:, l])
        xs, _ = _block(xs, c, lp, lam, lam_scale, ctx)
    y_prompt = _rms_norm(xp, final_norm_g)
    y_sample = _rms_norm(xs, final_norm_g)
    new_gqa_k = jnp.stack(gk, axis=1)
    new_gqa_v = jnp.stack(gv, axis=1)
    new_diff_k = jnp.stack(dk, axis=1)
    new_diff_v = jnp.stack(dv, axis=1)
    return (y_prompt, y_sample, new_gqa_k, new_gqa_v, new_diff_k, new_diff_v)
```

```python
import functools
import math

import numpy as np
import jax
import jax.numpy as jnp
from jax import lax
from jax.experimental import pallas as pl
from jax.experimental.pallas import tpu as pltpu

F32 = jnp.float32
BF16 = jnp.bfloat16

D_MODEL = 1024
BATCH = 32
SEQ = 256
DEPTH = 4
DEC_BATCH = 2
DEC_SEQ = 2048
PAST_LEN = 512
GRID_W = 64
ROPE_THETA = 10000.0
EPS = 1e-6
N_BRANCH = 4
BRANCH_W = 256
FNET_GW = 64
GQA_HEADS = 4
GQA_KV_HEADS = 2
GQA_HEAD_DIM = 64
DIFF_HEADS = 4
DIFF_V_DIM = 64
DIFF_QK_DIM = 32
SGU_GROUPS = 4
SGU_CHUNK = 128
D_FF = 4 * D_MODEL
N_MOD = 6
IN_W = 2048

CTX_ROWS = BATCH * SEQ
LAT_ROWS = DEC_BATCH * DEC_SEQ
ALL_ROWS = CTX_ROWS + LAT_ROWS
KEYS_LAT = PAST_LEN + DEC_SEQ

LANES = 128
VMEM_LIMIT = 52 * 1024 * 1024

CTX_SEQS_PER_STEP = 2
LAT_PRE_TM = 512
LAT_TQ = 256
ROW_TM = 512
COND_ROWS = 8

_A0, _QB0, _KB0, _VB0, _QC0, _KC0, _VC0, _DU0, _DV0 = 0, 256, 512, 640, 768, 1024, 1280, 1536, 1792
_Q_HEAD_ORDER = (0, 2, 1, 3)


def _dft_tables(length):
    k = np.arange(length, dtype=np.int64)
    idx = (k[:, None] * k[None, :]) % length
    ang = 2.0 * np.pi * idx.astype(np.float64) / length
    return np.concatenate([np.cos(ang), -np.sin(ang)], axis=1).astype(np.float32)


def _channel_dft_tables():
    k = np.arange(FNET_GW, dtype=np.int64)
    idx = (k[:, None] * k[None, :]) % FNET_GW
    ang = 2.0 * np.pi * idx.astype(np.float64) / FNET_GW
    c = np.cos(ang) / math.sqrt(FNET_GW)
    s = np.sin(ang) / math.sqrt(FNET_GW)
    eye = np.eye(BRANCH_W // FNET_GW)
    return (np.kron(eye, c).astype(np.float32), np.kron(eye, s).astype(np.float32))


def _rope_tables(length, dim):
    rows = length // GRID_W
    row = np.repeat(np.arange(rows, dtype=np.float64), GRID_W)
    col = np.tile(np.arange(GRID_W, dtype=np.float64), rows)
    quarter = dim // 4
    inv = ROPE_THETA ** (-np.arange(quarter, dtype=np.float64) / quarter)
    ang = np.concatenate([row[:, None] * inv, col[:, None] * inv], axis=-1)
    c = np.concatenate([np.cos(ang), np.cos(ang)], axis=-1)
    s = np.concatenate([-np.sin(ang), np.sin(ang)], axis=-1)
    reps = LANES // dim
    return (np.tile(c, (1, reps)).astype(np.float32), np.tile(s, (1, reps)).astype(np.float32))


def _dot(a, b):
    return jnp.dot(a, b, preferred_element_type=F32)


def _dot_nt(a, b):
    return lax.dot_general(a, b, (((1,), (1,)), ((), ())), preferred_element_type=F32)


def _rms_mod(x, g, scale, shift):
    y = x * lax.rsqrt(jnp.mean(x * x, axis=-1, keepdims=True) + EPS) * g
    return y * (1.0 + scale) + shift


def _gelu(x):
    c = math.sqrt(2.0 / math.pi)
    return 0.5 * x * (1.0 + jnp.tanh(c * (x + 0.044715 * (x * x * x))))


def _group_ones(width, group):
    r = lax.broadcasted_iota(jnp.int32, (width, width), 0) // group
    c = lax.broadcasted_iota(jnp.int32, (width, width), 1) // group
    return (r == c).astype(BF16)


def _group_mean_sq(x, group):
    x2 = x * x
    hi = x2.astype(BF16)
    lo = (x2 - hi.astype(F32)).astype(BF16)
    ones = _group_ones(x.shape[-1], group)
    return (_dot(hi, ones) + _dot(lo, ones)) * (1.0 / group)


def _lane_mask(lo, hi):
    lane = lax.broadcasted_iota(jnp.int32, (1, LANES), 1)
    return (lane >= lo) & (lane < hi)


def _swap_halves(x, dim):
    n = x.shape[-1]
    half = dim // 2
    lane = lax.broadcasted_iota(jnp.int32, (1, n), 1)
    fwd = pltpu.roll(x, n - half, 1)
    bwd = pltpu.roll(x, half, 1)
    return jnp.where((lane % dim) < half, fwd, bwd)


def _rope(x, cos, sin, dim):
    reps = x.shape[-1] // LANES
    if reps > 1:
        cos = jnp.concatenate([cos] * reps, axis=1)
        sin = jnp.concatenate([sin] * reps, axis=1)
    return x * cos + _swap_halves(x, dim) * sin


def _softmax_parts(s):
    m = jnp.max(s, axis=-1, keepdims=True)
    e = jnp.exp(s - m)
    return e, jnp.sum(e, axis=-1, keepdims=True)


def _gqa(q, k, v):
    outs = []
    for blk in range(2):
        q128 = q[:, blk * LANES:(blk + 1) * LANES]
        o_blk = None
        for kv in range(GQA_KV_HEADS):
            mask = _lane_mask(kv * GQA_HEAD_DIM, (kv + 1) * GQA_HEAD_DIM)
            qm = jnp.where(mask, q128, 0.0).astype(BF16)
            e, l = _softmax_parts(_dot_nt(qm, k))
            o = _dot(e.astype(BF16), v) / l
            o = jnp.where(mask, o, 0.0)
            o_blk = o if o_blk is None else o_blk + o
        outs.append(o_blk)
    return jnp.concatenate(outs, axis=1)


def _diff_attn(q, k, v, lam):
    outs = []
    for blk in range(2):
        q128 = q[:, blk * LANES:(blk + 1) * LANES]
        k128 = k[:, blk * LANES:(blk + 1) * LANES]
        v128 = v[:, blk * LANES:(blk + 1) * LANES]
        o_blk = None
        for hh in range(2):
            base = hh * DIFF_V_DIM
            probs = []
            for m in range(2):
                mask = _lane_mask(base + m * DIFF_QK_DIM, base + (m + 1) * DIFF_QK_DIM)
                qm = jnp.where(mask, q128, 0.0).astype(BF16)
                e, l = _softmax_parts(_dot_nt(qm, k128))
                probs.append((e, l))
            w = probs[0][0] * (1.0 / probs[0][1]) - probs[1][0] * (lam / probs[1][1])
            o = _dot(w.astype(BF16), v128)
            o = jnp.where(_lane_mask(base, base + DIFF_V_DIM), o, 0.0)
            o_blk = o if o_blk is None else o_blk + o
        outs.append(o_blk)
    return jnp.concatenate(outs, axis=1)


def _lambda(lamv, layer):
    lam_init = 0.8 - 0.6 * math.exp(-0.3 * layer)
    a = jnp.sum(lamv[0:1] * lamv[1:2], axis=-1, keepdims=True)
    b = jnp.sum(lamv[2:3] * lamv[3:4], axis=-1, keepdims=True)
    return jnp.exp(a) - jnp.exp(b) + lam_init, 1.0 - lam_init


def _diff_norm(oc, g_tiled, lam_scale):
    return oc * lax.rsqrt(_group_mean_sq(oc, DIFF_V_DIM) + EPS) * g_tiled * lam_scale


def _sgu(du, dvn, wsp, bias):
    lane = lax.broadcasted_iota(jnp.int32, (1, BRANCH_W), 1)
    outs = []
    for n in range(du.shape[0] // SGU_CHUNK):
        rows = slice(n * SGU_CHUNK, (n + 1) * SGU_CHUNK)
        r = _dot(wsp, dvn[rows].astype(BF16))
        s = bias
        for g in range(SGU_GROUPS):
            gm = (lane >= g * 64) & (lane < (g + 1) * 64)
            s = s + jnp.where(gm, r[g * SGU_CHUNK:(g + 1) * SGU_CHUNK], 0.0)
        outs.append(du[rows] * s)
    return outs[0] if len(outs) == 1 else jnp.concatenate(outs, axis=0)


def _fourier_stage1(a, bdc, bds):
    return jnp.concatenate([_dot(a, bdc), _dot(a, bds)], axis=0).astype(BF16)


def _fourier_stage2(dft_rows, f, length, w_fourier):
    y = _dot(dft_rows, f) * (1.0 / math.sqrt(length))
    return _dot(y.astype(BF16), w_fourier)


def _ada_kernel(cond_ref, w_ref, b_ref, o_ref):
    c = cond_ref[...]
    s = c * jax.nn.sigmoid(c)
    o_ref[...] = _dot(s.astype(BF16), w_ref[...].astype(BF16)) + b_ref[...]


def _ctx_kernel(layer, x_ref, mod_ref, g1_ref, win_ref, gq_ref, gk_ref, gsgu_ref, gdiff_ref,
                lamv_ref, dft_ref, bdc_ref, bds_ref, wf_ref, wsp_ref, bsp_ref,
                br_ref, ok_ref, ov_ref, odk_ref, odv_ref):
    x = x_ref[...]
    mod = mod_ref[...]
    h = _rms_mod(x, g1_ref[...], mod[1:2], mod[0:1]).astype(BF16)
    proj = _dot(h, win_ref[...])
    qb = proj[:, _QB0:_KB0]
    kb = proj[:, _KB0:_VB0]
    qb = qb * lax.rsqrt(_group_mean_sq(qb, GQA_HEAD_DIM) + EPS) * gq_ref[...]
    kb = kb * lax.rsqrt(_group_mean_sq(kb, GQA_HEAD_DIM) + EPS) * gk_ref[...]
    vb = proj[:, _VB0:_QC0]
    qc = proj[:, _QC0:_KC0]
    kc = proj[:, _KC0:_VC0]
    vc = proj[:, _VC0:_DU0]
    du = _gelu(proj[:, _DU0:_DV0])
    dv = _gelu(proj[:, _DV0:IN_W])
    dvn = dv * lax.rsqrt(jnp.mean(dv * dv, axis=-1, keepdims=True) + EPS) * gsgu_ref[...]
    lam, lam_scale = _lambda(lamv_ref[...], layer)
    qb = qb * (1.0 / math.sqrt(GQA_HEAD_DIM))
    qc = qc * (1.0 / math.sqrt(DIFF_QK_DIM))
    a = proj[:, _A0:_QB0].astype(BF16)
    for s in range(CTX_SEQS_PER_STEP):
        rows = slice(s * SEQ, (s + 1) * SEQ)
        ok_ref[s] = kb[rows]
        ov_ref[s] = vb[rows]
        odk_ref[s] = kc[rows]
        odv_ref[s] = vc[rows]
        f = _fourier_stage1(a[rows], bdc_ref[...], bds_ref[...])
        br_ref[rows, 0:256] = _fourier_stage2(dft_ref[...], f, SEQ, wf_ref[...]).astype(BF16)
        br_ref[rows, 256:512] = _gqa(qb[rows], kb[rows].astype(BF16), vb[rows].astype(BF16)).astype(BF16)
        oc = _diff_attn(qc[rows], kc[rows].astype(BF16), vc[rows].astype(BF16), lam)
        br_ref[rows, 512:768] = _diff_norm(oc, gdiff_ref[...], lam_scale).astype(BF16)
        br_ref[rows, 768:1024] = _sgu(du[rows], dvn[rows], wsp_ref[...], bsp_ref[...]).astype(BF16)


def _lat_pre_kernel(x_ref, mod_ref, g1_ref, win_ref, gq_ref, gk_ref, gsgu_ref,
                    cb_ref, sb_ref, cc_ref, sc_ref,
                    a_ref, qb_ref, kvb_ref, qc_ref, kvc_ref, du_ref, dvn_ref):
    x = x_ref[...]
    mod = mod_ref[...]
    h = _rms_mod(x, g1_ref[...], mod[1:2], mod[0:1]).astype(BF16)
    proj = _dot(h, win_ref[...])
    cb, sb, cc, sc = cb_ref[...], sb_ref[...], cc_ref[...], sc_ref[...]
    qb = proj[:, _QB0:_KB0]
    kb = proj[:, _KB0:_VB0]
    qb = qb * lax.rsqrt(_group_mean_sq(qb, GQA_HEAD_DIM) + EPS) * gq_ref[...]
    kb = kb * lax.rsqrt(_group_mean_sq(kb, GQA_HEAD_DIM) + EPS) * gk_ref[...]
    qb = _rope(qb, cb, sb, GQA_HEAD_DIM) * (1.0 / math.sqrt(GQA_HEAD_DIM))
    kb = _rope(kb, cb, sb, GQA_HEAD_DIM)
    qc = _rope(proj[:, _QC0:_KC0], cc, sc, DIFF_QK_DIM) * (1.0 / math.sqrt(DIFF_QK_DIM))
    kc = _rope(proj[:, _KC0:_VC0], cc, sc, DIFF_QK_DIM)
    dv = _gelu(proj[:, _DV0:IN_W])
    a_ref[...] = proj[:, _A0:_QB0].astype(BF16)
    qb_ref[...] = qb.astype(BF16)
    kvb_ref[:, 0:128] = kb.astype(BF16)
    kvb_ref[:, 128:256] = proj[:, _VB0:_QC0].astype(BF16)
    qc_ref[...] = qc.astype(BF16)
    kvc_ref[:, 0:256] = kc.astype(BF16)
    kvc_ref[:, 256:512] = proj[:, _VC0:_DU0].astype(BF16)
    du_ref[...] = _gelu(proj[:, _DU0:_DV0])
    dvn_ref[...] = (dv * lax.rsqrt(jnp.mean(dv * dv, axis=-1, keepdims=True) + EPS)
                    * gsgu_ref[...]).astype(BF16)


def _lat_mix_kernel(layer, br_in_ref, a_ref, qb_ref, kvb_ref, qc_ref, kvc_ref, du_ref, dvn_ref,
                    ck_ref, cv_ref, cdk_ref, cdv_ref, gdiff_ref, lamv_ref,
                    dft_ref, bdc_ref, bds_ref, wf_ref, wsp_ref, bsp_ref,
                    br_ref, kb_s, vb_s, kc_s, vc_s, f_s):
    del br_in_ref

    @pl.when(pl.program_id(1) == 0)
    def _fill():
        kb_s[0:PAST_LEN] = ck_ref[...].astype(BF16)
        vb_s[0:PAST_LEN] = cv_ref[...].astype(BF16)
        kc_s[0:PAST_LEN] = cdk_ref[...].astype(BF16)
        vc_s[0:PAST_LEN] = cdv_ref[...].astype(BF16)
        kb_s[PAST_LEN:KEYS_LAT] = kvb_ref[:, 0:128]
        vb_s[PAST_LEN:KEYS_LAT] = kvb_ref[:, 128:256]
        kc_s[PAST_LEN:KEYS_LAT] = kvc_ref[:, 0:256]
        vc_s[PAST_LEN:KEYS_LAT] = kvc_ref[:, 256:512]
        f_s[...] = _fourier_stage1(a_ref[...], bdc_ref[...], bds_ref[...])

    lam, lam_scale = _lambda(lamv_ref[...], layer)
    br_ref[:, 0:256] = _fourier_stage2(dft_ref[...], f_s[...], DEC_SEQ, wf_ref[...]).astype(BF16)
    br_ref[:, 256:512] = _gqa(qb_ref[...].astype(F32), kb_s[...], vb_s[...]).astype(BF16)
    oc = _diff_attn(qc_ref[...].astype(F32), kc_s[...], vc_s[...], lam)
    br_ref[:, 512:768] = _diff_norm(oc, gdiff_ref[...], lam_scale).astype(BF16)
    br_ref[:, 768:1024] = _sgu(du_ref[...], dvn_ref[...].astype(F32), wsp_ref[...],
                               bsp_ref[...]).astype(BF16)


def _merge_kernel(x_ref, br_ref, mod_ref, g1_ref, wg_ref, wb_ref, wo_ref, o_ref):
    x = x_ref[...]
    mod = mod_ref[...]
    h = _rms_mod(x, g1_ref[...], mod[1:2], mod[0:1]).astype(BF16)
    merged = None
    for n in range(N_BRANCH):
        gate = jax.nn.sigmoid(_dot(h, wg_ref[:, n * D_MODEL:(n + 1) * D_MODEL]))
        term = gate * _dot(br_ref[:, n * BRANCH_W:(n + 1) * BRANCH_W], wb_ref[n])
        merged = term if merged is None else merged + term
    o_ref[...] = x + mod[2:3] * _dot(merged.astype(BF16), wo_ref[...])


def _mlp_kernel(x_ref, mod_ref, g2_ref, w1_ref, w2_ref, o_ref):
    x = x_ref[...]
    mod = mod_ref[...]
    h = _rms_mod(x, g2_ref[...], mod[4:5], mod[3:4]).astype(BF16)
    acc = None
    for n in range(D_FF // D_MODEL):
        cols = slice(n * D_MODEL, (n + 1) * D_MODEL)
        hid = jnp.maximum(_dot(h, w1_ref[:, cols]), 0.0)
        term = _dot((hid * hid).astype(BF16), w2_ref[cols, :])
        acc = term if acc is None else acc + term
    o_ref[...] = x + mod[5:6] * acc


def _final_norm_kernel(x_ref, g_ref, o_ref):
    x = x_ref[...]
    o_ref[...] = x * lax.rsqrt(jnp.mean(x * x, axis=-1, keepdims=True) + EPS) * g_ref[...]


def _params(*sem):
    return pltpu.CompilerParams(dimension_semantics=sem, vmem_limit_bytes=VMEM_LIMIT)


def _full(shape):
    nd = len(shape)
    return pl.BlockSpec(shape, lambda *_: (0,) * nd)


def _layer_block(shape, layer):
    nd = len(shape)
    return pl.BlockSpec((None,) + tuple(shape), lambda *_: (layer,) + (0,) * nd)


def _row_mod_index(layer, tm):
    ctx_tiles = CTX_ROWS // tm
    tiles_per_seq = DEC_SEQ // tm

    def index(i):
        sel = jnp.where(i < ctx_tiles, 0, 1 + (i - ctx_tiles) // tiles_per_seq)
        return (layer, sel, 0, 0)
    return index


def _ada_call(cond, w_ada, b_ada):
    return pl.pallas_call(
        _ada_kernel,
        grid=(DEPTH, N_MOD),
        in_specs=[
            pl.BlockSpec((COND_ROWS, D_MODEL), lambda l, j: (0, 0)),
            pl.BlockSpec((None, D_MODEL, D_MODEL), lambda l, j: (l, 0, j)),
            pl.BlockSpec((None, 1, D_MODEL), lambda l, j: (l, 0, j)),
        ],
        out_specs=pl.BlockSpec((None, COND_ROWS, D_MODEL), lambda l, j: (l, 0, j)),
        out_shape=jax.ShapeDtypeStruct((DEPTH, COND_ROWS, N_MOD * D_MODEL), F32),
        compiler_params=_params("arbitrary", "arbitrary"),
        name="ada",
    )(cond, w_ada, b_ada.reshape(DEPTH, 1, N_MOD * D_MODEL))


def _ctx_call(layer, x, mods, p, caches):
    ns = CTX_SEQS_PER_STEP
    tm = ns * SEQ
    cache_shapes = [(BATCH, DEPTH, SEQ, 128), (BATCH, DEPTH, SEQ, 128),
                    (BATCH, DEPTH, SEQ, 256), (BATCH, DEPTH, SEQ, 256)]
    in_specs = [
        pl.BlockSpec((tm, D_MODEL), lambda i: (i, 0)),
        pl.BlockSpec((None, None, N_MOD, D_MODEL), lambda i: (layer, 0, 0, 0)),
        _layer_block((1, D_MODEL), layer),
        _layer_block((D_MODEL, IN_W), layer),
        _layer_block((1, 256), layer),
        _layer_block((1, 128), layer),
        _layer_block((1, 256), layer),
        _layer_block((1, 256), layer),
        _layer_block((4, DIFF_QK_DIM), layer),
        _full((SEQ, 2 * SEQ)),
        _full((BRANCH_W, BRANCH_W)),
        _full((BRANCH_W, BRANCH_W)),
        _layer_block((BRANCH_W, BRANCH_W), layer),
        _layer_block((SGU_GROUPS * SGU_CHUNK, SGU_CHUNK), layer),
        _layer_block((SGU_CHUNK, BRANCH_W), layer),
    ]
    args = [x, mods, p["norm1_g"], p["w_in"], p["gq"], p["gk"], p["gsgu"], p["gdiff"], p["lamv"],
            p["dft_ctx"], p["bdc"], p["bds"], p["w_fourier"], p["wsp"], p["bsp"]]
    aliases = {}
    if caches is not None:
        for j, cache in enumerate(caches):
            in_specs.append(pl.BlockSpec(memory_space=pl.ANY))
            aliases[len(args)] = 1 + j
            args.append(cache)
    kernel = functools.partial(_ctx_kernel, layer)
    if caches is not None:
        base = kernel

        def kernel(*refs):
            n_in = 15
            return base(*refs[:n_in], *refs[n_in + 4:])
    out_specs = [pl.BlockSpec((tm, D_MODEL), lambda i: (i, 0))] + [
        pl.BlockSpec((ns, None, SEQ, s[-1]), lambda i: (i, layer, 0, 0)) for s in cache_shapes]
    out_shape = [jax.ShapeDtypeStruct((ALL_ROWS, D_MODEL), BF16)] + [
        jax.ShapeDtypeStruct(s, F32) for s in cache_shapes]
    outs = pl.pallas_call(
        kernel,
        grid=(BATCH // ns,),
        in_specs=in_specs,
        out_specs=out_specs,
        out_shape=out_shape,
        input_output_aliases=aliases,
        compiler_params=_params("arbitrary"),
        name=f"ctx_mix_{layer}",
    )(*args)
    return outs[0], tuple(outs[1:])


def _lat_pre_call(layer, x, mods, p):
    tm = LAT_PRE_TM
    ctx_tiles = CTX_ROWS // tm
    tiles_per_seq = DEC_SEQ // tm
    rope_spec = pl.BlockSpec((tm, LANES), lambda i: (i % tiles_per_seq, 0))
    widths = [(256, BF16), (256, BF16), (256, BF16), (256, BF16), (512, BF16), (256, F32), (256, BF16)]
    return pl.pallas_call(
        _lat_pre_kernel,
        grid=(LAT_ROWS // tm,),
        in_specs=[
            pl.BlockSpec((tm, D_MODEL), lambda i: (ctx_tiles + i, 0)),
            pl.BlockSpec((None, None, N_MOD, D_MODEL), lambda i: (layer, 1 + i // tiles_per_seq, 0, 0)),
            _layer_block((1, D_MODEL), layer),
            _layer_block((D_MODEL, IN_W), layer),
            _layer_block((1, 256), layer),
            _layer_block((1, 128), layer),
            _layer_block((1, 256), layer),
            rope_spec, rope_spec, rope_spec, rope_spec,
        ],
        out_specs=[pl.BlockSpec((tm, w), lambda i: (i, 0)) for w, _ in widths],
        out_shape=[jax.ShapeDtypeStruct((LAT_ROWS, w), dt) for w, dt in widths],
        compiler_params=_params("arbitrary"),
        name=f"lat_pre_{layer}",
    )(x, mods, p["norm1_g"], p["w_in"], p["gq"], p["gk"], p["gsgu"],
      p["rope_cb"], p["rope_sb"], p["rope_cc"], p["rope_sc"])


def _lat_mix_call(layer, br, pre, caches, p):
    tq = LAT_TQ
    nq = DEC_SEQ // tq
    ctx_tiles = CTX_ROWS // tq
    a, qb, kvb, qc, kvc, du, dvn = pre
    seq_block = lambda w: pl.BlockSpec((DEC_SEQ, w), lambda b, q: (b, 0))
    q_block = lambda w: pl.BlockSpec((tq, w), lambda b, q: (b * nq + q, 0))
    cache_block = lambda w: pl.BlockSpec((None, None, PAST_LEN, w), lambda b, q: (b, layer, 0, 0))
    return pl.pallas_call(
        functools.partial(_lat_mix_kernel, layer),
        grid=(DEC_BATCH, nq),
        in_specs=[
            pl.BlockSpec(memory_space=pl.ANY),
            seq_block(256), q_block(256), seq_block(256), q_block(256), seq_block(512),
            q_block(256), q_block(256),
            cache_block(128), cache_block(128), cache_block(256), cache_block(256),
            _layer_block((1, 256), layer),
            _layer_block((4, DIFF_QK_DIM), layer),
            pl.BlockSpec((tq, 2 * DEC_SEQ), lambda b, q: (q, 0)),
            _full((BRANCH_W, BRANCH_W)),
            _full((BRANCH_W, BRANCH_W)),
            _layer_block((BRANCH_W, BRANCH_W), layer),
            _layer_block((SGU_GROUPS * SGU_CHUNK, SGU_CHUNK), layer),
            _layer_block((SGU_CHUNK, BRANCH_W), layer),
        ],
        out_specs=pl.BlockSpec((tq, D_MODEL), lambda b, q: (ctx_tiles + b * nq + q, 0)),
        out_shape=jax.ShapeDtypeStruct((ALL_ROWS, D_MODEL), BF16),
        scratch_shapes=[
            pltpu.VMEM((KEYS_LAT, 128), BF16), pltpu.VMEM((KEYS_LAT, 128), BF16),
            pltpu.VMEM((KEYS_LAT, 256), BF16), pltpu.VMEM((KEYS_LAT, 256), BF16),
            pltpu.VMEM((2 * DEC_SEQ, BRANCH_W), BF16),
        ],
        input_output_aliases={0: 0},
        compiler_params=_params("arbitrary", "arbitrary"),
        name=f"lat_mix_{layer}",
    )(br, a, qb, kvb, qc, kvc, du, dvn, *caches, p["gdiff"], p["lamv"],
      p["dft_lat"], p["bdc"], p["bds"], p["w_fourier"], p["wsp"], p["bsp"])


def _merge_call(layer, x, br, mods, p):
    tm = ROW_TM
    return pl.pallas_call(
        _merge_kernel,
        grid=(ALL_ROWS // tm,),
        in_specs=[
            pl.BlockSpec((tm, D_MODEL), lambda i: (i, 0)),
            pl.BlockSpec((tm, D_MODEL), lambda i: (i, 0)),
            pl.BlockSpec((None, None, N_MOD, D_MODEL), _row_mod_index(layer, tm)),
            _layer_block((1, D_MODEL), layer),
            _layer_block((D_MODEL, N_BRANCH * D_MODEL), layer),
            _layer_block((N_BRANCH, BRANCH_W, D_MODEL), layer),
            _layer_block((D_MODEL, D_MODEL), layer),
        ],
        out_specs=pl.BlockSpec((tm, D_MODEL), lambda i: (i, 0)),
        out_shape=jax.ShapeDtypeStruct((ALL_ROWS, D_MODEL), F32),
        compiler_params=_params("arbitrary"),
        name=f"merge_{layer}",
    )(x, br, mods, p["norm1_g"], p["w_gate"], p["w_branch"], p["w_out"])


def _mlp_call(layer, x, mods, p):
    tm = ROW_TM
    return pl.pallas_call(
        _mlp_kernel,
        grid=(ALL_ROWS // tm,),
        in_specs=[
            pl.BlockSpec((tm, D_MODEL), lambda i: (i, 0)),
            pl.BlockSpec((None, None, N_MOD, D_MODEL), _row_mod_index(layer, tm)),
            _layer_block((1, D_MODEL), layer),
            _layer_block((D_MODEL, D_FF), layer),
            _layer_block((D_FF, D_MODEL), layer),
        ],
        out_specs=pl.BlockSpec((tm, D_MODEL), lambda i: (i, 0)),
        out_shape=jax.ShapeDtypeStruct((ALL_ROWS, D_MODEL), F32),
        compiler_params=_params("arbitrary"),
        name=f"mlp_{layer}",
    )(x, mods, p["norm2_g"], p["w_mlp1"], p["w_mlp2"])


def _final_norm_call(x, g):
    tm = ROW_TM
    return pl.pallas_call(
        _final_norm_kernel,
        grid=(ALL_ROWS // tm,),
        in_specs=[pl.BlockSpec((tm, D_MODEL), lambda i: (i, 0)), _full((1, D_MODEL))],
        out_specs=pl.BlockSpec((tm, D_MODEL), lambda i: (i, 0)),
        out_shape=jax.ShapeDtypeStruct((ALL_ROWS, D_MODEL), F32),
        compiler_params=_params("arbitrary"),
        name="final_norm",
    )(x, g)


def _permute_q_heads(w, axis):
    shape = w.shape
    w = w.reshape(shape[:axis] + (GQA_HEADS, GQA_HEAD_DIM) + shape[axis + 1:])
    w = jnp.take(w, jnp.array(_Q_HEAD_ORDER), axis=axis)
    return w.reshape(shape)


def kernel(x_prompt, x_sample, c, cache_gqa_k, cache_gqa_v, cache_diff_k, cache_diff_v, c_ctx, w_ada, b_ada, norm1_g, norm2_g, w_in, w_fourier, q_norm_g, k_norm_g, lambda_q1, lambda_k1, lambda_q2, lambda_k2, diff_norm_g, sgu_norm_g, w_spatial, b_spatial, w_gate, w_branch, w_out, w_mlp1, w_mlp2, final_norm_g):
    w_in_p = jnp.concatenate(
        [w_in[..., :_QB0], _permute_q_heads(w_in[..., _QB0:_KB0], 2), w_in[..., _KB0:]], axis=-1)
    w_branch_p = jnp.concatenate(
        [w_branch[:, 0:1], _permute_q_heads(w_branch[:, 1:2], 2), w_branch[:, 2:]], axis=1)
    bdc, bds = _channel_dft_tables()
    cb, sb = _rope_tables(DEC_SEQ, GQA_HEAD_DIM)
    cc, sc = _rope_tables(DEC_SEQ, DIFF_QK_DIM)
    p = {
        "norm1_g": norm1_g.reshape(DEPTH, 1, D_MODEL),
        "norm2_g": norm2_g.reshape(DEPTH, 1, D_MODEL),
        "w_in": w_in_p.astype(BF16),
        "w_fourier": w_fourier.astype(BF16),
        "gq": jnp.tile(q_norm_g, (1, GQA_HEADS)).reshape(DEPTH, 1, 256),
        "gk": jnp.tile(k_norm_g, (1, GQA_KV_HEADS)).reshape(DEPTH, 1, 128),
        "gsgu": sgu_norm_g.reshape(DEPTH, 1, BRANCH_W),
        "gdiff": jnp.tile(diff_norm_g, (1, DIFF_HEADS)).reshape(DEPTH, 1, 256),
        "lamv": jnp.stack([lambda_q1, lambda_k1, lambda_q2, lambda_k2], axis=1),
        "wsp": w_spatial.reshape(DEPTH, SGU_GROUPS * SGU_CHUNK, SGU_CHUNK).astype(BF16),
        "bsp": jnp.repeat(jnp.swapaxes(b_spatial, 1, 2), BRANCH_W // SGU_GROUPS, axis=2),
        "w_gate": w_gate.astype(BF16),
        "w_branch": w_branch_p.astype(BF16),
        "w_out": w_out.astype(BF16),
        "w_mlp1": w_mlp1.astype(BF16),
        "w_mlp2": w_mlp2.astype(BF16),
        "dft_ctx": jnp.asarray(_dft_tables(SEQ)).astype(BF16),
        "dft_lat": jnp.asarray(_dft_tables(DEC_SEQ)).astype(BF16),
        "bdc": jnp.asarray(bdc).astype(BF16),
        "bds": jnp.asarray(bds).astype(BF16),
        "rope_cb": jnp.asarray(cb), "rope_sb": jnp.asarray(sb),
        "rope_cc": jnp.asarray(cc), "rope_sc": jnp.asarray(sc),
    }
    lat_caches = (
        cache_gqa_k.reshape(DEC_BATCH, DEPTH, PAST_LEN, 128),
        cache_gqa_v.reshape(DEC_BATCH, DEPTH, PAST_LEN, 128),
        cache_diff_k.reshape(DEC_BATCH, DEPTH, PAST_LEN, 256),
        cache_diff_v.reshape(DEC_BATCH, DEPTH, PAST_LEN, 256),
    )

    cond = jnp.concatenate(
        [c_ctx[None, :], c, jnp.zeros((COND_ROWS - 1 - DEC_BATCH, D_MODEL), F32)], axis=0)
    mods = _ada_call(cond, w_ada, b_ada)[:, :1 + DEC_BATCH].reshape(
        DEPTH, 1 + DEC_BATCH, N_MOD, D_MODEL)

    x = jnp.concatenate(
        [x_prompt.reshape(CTX_ROWS, D_MODEL), x_sample.reshape(LAT_ROWS, D_MODEL)], axis=0)
    new_caches = None
    for layer in range(DEPTH):
        br, new_caches = _ctx_call(layer, x, mods, p, new_caches)
        pre = _lat_pre_call(layer, x, mods, p)
        br = _lat_mix_call(layer, br, pre, lat_caches, p)
        x = _merge_call(layer, x, br, mods, p)
        x = _mlp_call(layer, x, mods, p)
    y = _final_norm_call(x, final_norm_g.reshape(1, D_MODEL))

    gk, gv, dk, dv = new_caches
    return (
        y[:CTX_ROWS].reshape(BATCH, SEQ, D_MODEL),
        y[CTX_ROWS:].reshape(DEC_BATCH, DEC_SEQ, D_MODEL),
        gk.reshape(BATCH, DEPTH, SEQ, GQA_KV_HEADS, GQA_HEAD_DIM),
        gv.reshape(BATCH, DEPTH, SEQ, GQA_KV_HEADS, GQA_HEAD_DIM),
        dk.reshape(BATCH, DEPTH, SEQ, DIFF_HEADS, 2, DIFF_QK_DIM),
        dv.reshape(BATCH, DEPTH, SEQ, DIFF_HEADS, DIFF_V_DIM),
    )
```

```python
import functools
import math

import numpy as np
import jax
import jax.numpy as jnp
from jax import lax
from jax.experimental import pallas as pl
from jax.experimental.pallas import tpu as pltpu

F32 = jnp.float32
BF16 = jnp.bfloat16

D_MODEL = 1024
BATCH = 32
SEQ = 256
DEPTH = 4
DEC_BATCH = 2
DEC_SEQ = 2048
PAST_LEN = 512
GRID_W = 64
ROPE_THETA = 10000.0
EPS = 1e-6
N_BRANCH = 4
BRANCH_W = 256
FNET_GW = 64
GQA_HEADS = 4
GQA_KV_HEADS = 2
GQA_HEAD_DIM = 64
DIFF_HEADS = 4
DIFF_V_DIM = 64
DIFF_QK_DIM = 32
SGU_GROUPS = 4
SGU_CHUNK = 128
D_FF = 4 * D_MODEL
N_MOD = 6
IN_W = 2048

CTX_ROWS = BATCH * SEQ
LAT_ROWS = DEC_BATCH * DEC_SEQ
ALL_ROWS = CTX_ROWS + LAT_ROWS
KEYS_LAT = PAST_LEN + DEC_SEQ

LANES = 128
VMEM_LIMIT = 52 * 1024 * 1024

CTX_SEQS_PER_STEP = 2
LAT_PRE_TM = 512
LAT_TQ = 256
ROW_TM = 512
COND_ROWS = 8

_A0, _QB0, _KB0, _VB0, _QC0, _KC0, _VC0, _DU0, _DV0 = 0, 256, 512, 640, 768, 1024, 1280, 1536, 1792
_Q_HEAD_ORDER = (0, 2, 1, 3)
_LOG2E = math.log2(math.e)
_GQA_Q_SCALE = _LOG2E / math.sqrt(GQA_HEAD_DIM)
_DIFF_Q_SCALE = _LOG2E / math.sqrt(DIFF_QK_DIM)


def _dft_tables(length):
    k = np.arange(length, dtype=np.int64)
    idx = (k[:, None] * k[None, :]) % length
    ang = 2.0 * np.pi * idx.astype(np.float64) / length
    return np.concatenate([np.cos(ang), -np.sin(ang)], axis=1).astype(np.float32)


def _channel_dft_tables():
    k = np.arange(FNET_GW, dtype=np.int64)
    idx = (k[:, None] * k[None, :]) % FNET_GW
    ang = 2.0 * np.pi * idx.astype(np.float64) / FNET_GW
    c = np.cos(ang) / math.sqrt(FNET_GW)
    s = np.sin(ang) / math.sqrt(FNET_GW)
    eye = np.eye(BRANCH_W // FNET_GW)
    return (np.kron(eye, c).astype(np.float32), np.kron(eye, s).astype(np.float32))


def _rope_tables(length, dim):
    rows = length // GRID_W
    row = np.repeat(np.arange(rows, dtype=np.float64), GRID_W)
    col = np.tile(np.arange(GRID_W, dtype=np.float64), rows)
    quarter = dim // 4
    inv = ROPE_THETA ** (-np.arange(quarter, dtype=np.float64) / quarter)
    ang = np.concatenate([row[:, None] * inv, col[:, None] * inv], axis=-1)
    c = np.concatenate([np.cos(ang), np.cos(ang)], axis=-1)
    s = np.concatenate([-np.sin(ang), np.sin(ang)], axis=-1)
    reps = LANES // dim
    return (np.tile(c, (1, reps)).astype(np.float32), np.tile(s, (1, reps)).astype(np.float32))


def _dot(a, b):
    return jnp.dot(a, b, preferred_element_type=F32)


def _dot_nt(a, b):
    return lax.dot_general(a, b, (((1,), (1,)), ((), ())), preferred_element_type=F32)


def _rms_mod(x, g, scale, shift):
    y = x * lax.rsqrt(jnp.mean(x * x, axis=-1, keepdims=True) + EPS) * g
    return y * (1.0 + scale) + shift


def _gelu(x):
    c = math.sqrt(2.0 / math.pi)
    return 0.5 * x * (1.0 + jnp.tanh(c * (x + 0.044715 * (x * x * x))))


def _group_ones(width, group):
    r = lax.broadcasted_iota(jnp.int32, (width, width), 0) // group
    c = lax.broadcasted_iota(jnp.int32, (width, width), 1) // group
    return (r == c).astype(BF16)


def _group_mean_sq(x, group):
    x2 = x * x
    hi = x2.astype(BF16)
    lo = (x2 - hi.astype(F32)).astype(BF16)
    ones = _group_ones(x.shape[-1], group)
    return (_dot(hi, ones) + _dot(lo, ones)) * (1.0 / group)


def _lane_mask(lo, hi):
    lane = lax.broadcasted_iota(jnp.int32, (1, LANES), 1)
    return (lane >= lo) & (lane < hi)


def _swap_halves(x, dim):
    n = x.shape[-1]
    half = dim // 2
    lane = lax.broadcasted_iota(jnp.int32, (1, n), 1)
    fwd = pltpu.roll(x, n - half, 1)
    bwd = pltpu.roll(x, half, 1)
    return jnp.where((lane % dim) < half, fwd, bwd)


def _rope(x, cos, sin, dim):
    reps = x.shape[-1] // LANES
    if reps > 1:
        cos = jnp.concatenate([cos] * reps, axis=1)
        sin = jnp.concatenate([sin] * reps, axis=1)
    return x * cos + _swap_halves(x, dim) * sin


def _softmax_keys(s):
    m = jnp.max(s, axis=0, keepdims=True)
    e = jnp.exp2(s - m)
    return e, jnp.sum(e, axis=0, keepdims=True)


def _masked_scores(k_blk, q_blk, lo, hi):
    qm = jnp.where(_lane_mask(lo, hi), q_blk, 0.0).astype(BF16)
    return _dot_nt(k_blk, qm)


def _gqa(q, k, vt):
    outs = [None] * GQA_HEADS
    for pos, head in enumerate(_Q_HEAD_ORDER):
        blk, kv = pos // 2, pos % 2
        s = _masked_scores(k, q[:, blk * LANES:(blk + 1) * LANES],
                           kv * GQA_HEAD_DIM, (kv + 1) * GQA_HEAD_DIM)
        e, l = _softmax_keys(s)
        o = _dot(vt[kv * GQA_HEAD_DIM:(kv + 1) * GQA_HEAD_DIM, :], e.astype(BF16))
        outs[head] = o * (1.0 / l)
    return jnp.concatenate(outs, axis=0)


def _diff_attn(q, k, vt, lam):
    outs = []
    for head in range(DIFF_HEADS):
        blk, base = head // 2, (head % 2) * DIFF_V_DIM
        k_blk = k[:, blk * LANES:(blk + 1) * LANES]
        q_blk = q[:, blk * LANES:(blk + 1) * LANES]
        e1, l1 = _softmax_keys(_masked_scores(k_blk, q_blk, base, base + DIFF_QK_DIM))
        e2, l2 = _softmax_keys(_masked_scores(k_blk, q_blk, base + DIFF_QK_DIM, base + DIFF_V_DIM))
        w = e1 * (1.0 / l1) - e2 * (lam / l2)
        o = _dot(vt[head * DIFF_V_DIM:(head + 1) * DIFF_V_DIM, :], w.astype(BF16))
        outs.append(o * lax.rsqrt(jnp.mean(o * o, axis=0, keepdims=True) + EPS))
    return jnp.concatenate(outs, axis=0)


def _lambda(lamv, layer):
    lam_init = 0.8 - 0.6 * math.exp(-0.3 * layer)
    a = jnp.sum(lamv[0:1] * lamv[1:2], axis=-1, keepdims=True)
    b = jnp.sum(lamv[2:3] * lamv[3:4], axis=-1, keepdims=True)
    return jnp.exp(a) - jnp.exp(b) + lam_init, 1.0 - lam_init


def _sgu(du, dvn, wsp, bias):
    lane = lax.broadcasted_iota(jnp.int32, (1, BRANCH_W), 1)
    outs = []
    for n in range(du.shape[0] // SGU_CHUNK):
        rows = slice(n * SGU_CHUNK, (n + 1) * SGU_CHUNK)
        r = _dot(wsp, dvn[rows].astype(BF16))
        s = bias
        for g in range(SGU_GROUPS):
            gm = (lane >= g * 64) & (lane < (g + 1) * 64)
            s = s + jnp.where(gm, r[g * SGU_CHUNK:(g + 1) * SGU_CHUNK], 0.0)
        outs.append(du[rows] * s)
    return outs[0] if len(outs) == 1 else jnp.concatenate(outs, axis=0)


def _fourier_stage1(a, bdc, bds):
    return jnp.concatenate([_dot(a, bdc), _dot(a, bds)], axis=0).astype(BF16)


def _fourier_stage2(dft_rows, f, length, w_fourier):
    y = _dot(dft_rows, f) * (1.0 / math.sqrt(length))
    return _dot(y.astype(BF16), w_fourier)


def _ada_kernel(cond_ref, w_ref, b_ref, o_ref):
    c = cond_ref[...]
    s = c * jax.nn.sigmoid(c)
    o_ref[...] = _dot(s.astype(BF16), w_ref[...].astype(BF16)) + b_ref[...]


def _ctx_kernel(layer, x_ref, mod_ref, g1_ref, win_ref, gq_ref, gk_ref, gsgu_ref, gdiff_ref,
                lamv_ref, dft_ref, bdc_ref, bds_ref, wf_ref, wsp_ref, bsp_ref,
                br_ref, ok_ref, ov_ref, odk_ref, odv_ref):
    x = x_ref[...]
    mod = mod_ref[...]
    h = _rms_mod(x, g1_ref[...], mod[1:2], mod[0:1]).astype(BF16)
    proj = _dot(h, win_ref[...])
    qb = proj[:, _QB0:_KB0]
    kb = proj[:, _KB0:_VB0]
    qb = qb * lax.rsqrt(_group_mean_sq(qb, GQA_HEAD_DIM) + EPS) * gq_ref[...]
    kb = kb * lax.rsqrt(_group_mean_sq(kb, GQA_HEAD_DIM) + EPS) * gk_ref[...]
    vb = proj[:, _VB0:_QC0]
    qc = proj[:, _QC0:_KC0]
    kc = proj[:, _KC0:_VC0]
    vc = proj[:, _VC0:_DU0]
    du = _gelu(proj[:, _DU0:_DV0])
    dv = _gelu(proj[:, _DV0:IN_W])
    dvn = dv * lax.rsqrt(jnp.mean(dv * dv, axis=-1, keepdims=True) + EPS) * gsgu_ref[...]
    lam, lam_scale = _lambda(lamv_ref[...], layer)
    qb = qb * _GQA_Q_SCALE
    qc = qc * _DIFF_Q_SCALE
    a = proj[:, _A0:_QB0].astype(BF16)
    for s in range(CTX_SEQS_PER_STEP):
        rows = slice(s * SEQ, (s + 1) * SEQ)
        ok_ref[s] = kb[rows]
        ov_ref[s] = vb[rows]
        odk_ref[s] = kc[rows]
        odv_ref[s] = vc[rows]
        f = _fourier_stage1(a[rows], bdc_ref[...], bds_ref[...])
        br_ref[rows, 0:256] = _fourier_stage2(dft_ref[...], f, SEQ, wf_ref[...]).astype(BF16)
        ob = _gqa(qb[rows], kb[rows].astype(BF16), vb[rows].T.astype(BF16))
        br_ref[rows, 256:512] = ob.T.astype(BF16)
        oc = _diff_attn(qc[rows], kc[rows].astype(BF16), vc[rows].T.astype(BF16), lam)
        br_ref[rows, 512:768] = (oc.T * (gdiff_ref[...] * lam_scale)).astype(BF16)
        br_ref[rows, 768:1024] = _sgu(du[rows], dvn[rows], wsp_ref[...], bsp_ref[...]).astype(BF16)


def _lat_pre_kernel(x_ref, mod_ref, g1_ref, win_ref, gq_ref, gk_ref, gsgu_ref,
                    cb_ref, sb_ref, cc_ref, sc_ref,
                    a_ref, qb_ref, kb_ref, vbt_ref, qc_ref, kc_ref, vct_ref, du_ref, dvn_ref):
    x = x_ref[...]
    mod = mod_ref[...]
    h = _rms_mod(x, g1_ref[...], mod[1:2], mod[0:1]).astype(BF16)
    proj = _dot(h, win_ref[...])
    cb, sb, cc, sc = cb_ref[...], sb_ref[...], cc_ref[...], sc_ref[...]
    qb = proj[:, _QB0:_KB0]
    kb = proj[:, _KB0:_VB0]
    qb = qb * lax.rsqrt(_group_mean_sq(qb, GQA_HEAD_DIM) + EPS) * gq_ref[...]
    kb = kb * lax.rsqrt(_group_mean_sq(kb, GQA_HEAD_DIM) + EPS) * gk_ref[...]
    qb = _rope(qb, cb, sb, GQA_HEAD_DIM) * _GQA_Q_SCALE
    kb = _rope(kb, cb, sb, GQA_HEAD_DIM)
    qc = _rope(proj[:, _QC0:_KC0], cc, sc, DIFF_QK_DIM) * _DIFF_Q_SCALE
    kc = _rope(proj[:, _KC0:_VC0], cc, sc, DIFF_QK_DIM)
    dv = _gelu(proj[:, _DV0:IN_W])
    a_ref[...] = proj[:, _A0:_QB0].astype(BF16)
    qb_ref[...] = qb.astype(BF16)
    kb_ref[...] = kb.astype(BF16)
    vbt_ref[...] = proj[:, _VB0:_QC0].T.astype(BF16)
    qc_ref[...] = qc.astype(BF16)
    kc_ref[...] = kc.astype(BF16)
    vct_ref[...] = proj[:, _VC0:_DU0].T.astype(BF16)
    du_ref[...] = _gelu(proj[:, _DU0:_DV0])
    dvn_ref[...] = (dv * lax.rsqrt(jnp.mean(dv * dv, axis=-1, keepdims=True) + EPS)
                    * gsgu_ref[...]).astype(BF16)


def _lat_mix_kernel(layer, br_in_ref, a_ref, qb_ref, kb_ref, vbt_ref, qc_ref, kc_ref, vct_ref,
                    du_ref, dvn_ref, ck_ref, cv_ref, cdk_ref, cdv_ref, gdiff_ref, lamv_ref,
                    dft_ref, bdc_ref, bds_ref, wf_ref, wsp_ref, bsp_ref,
                    br_ref, kb_s, vbt_s, kc_s, vct_s, f_s):
    del br_in_ref

    @pl.when(pl.program_id(1) == 0)
    def _fill():
        kb_s[0:PAST_LEN] = ck_ref[...].astype(BF16)
        kc_s[0:PAST_LEN] = cdk_ref[...].astype(BF16)
        vbt_s[:, 0:PAST_LEN] = cv_ref[...].T.astype(BF16)
        vct_s[:, 0:PAST_LEN] = cdv_ref[...].T.astype(BF16)
        kb_s[PAST_LEN:KEYS_LAT] = kb_ref[...]
        kc_s[PAST_LEN:KEYS_LAT] = kc_ref[...]
        vbt_s[:, PAST_LEN:KEYS_LAT] = vbt_ref[...]
        vct_s[:, PAST_LEN:KEYS_LAT] = vct_ref[...]
        f_s[...] = _fourier_stage1(a_ref[...], bdc_ref[...], bds_ref[...])

    lam, lam_scale = _lambda(lamv_ref[...], layer)
    br_ref[:, 0:256] = _fourier_stage2(dft_ref[...], f_s[...], DEC_SEQ, wf_ref[...]).astype(BF16)
    ob = _gqa(qb_ref[...].astype(F32), kb_s[...], vbt_s[...])
    br_ref[:, 256:512] = ob.T.astype(BF16)
    oc = _diff_attn(qc_ref[...].astype(F32), kc_s[...], vct_s[...], lam)
    br_ref[:, 512:768] = (oc.T * (gdiff_ref[...] * lam_scale)).astype(BF16)
    br_ref[:, 768:1024] = _sgu(du_ref[...], dvn_ref[...].astype(F32), wsp_ref[...],
                               bsp_ref[...]).astype(BF16)


def _merge_kernel(xc_ref, xl_ref, br_ref, mod_ref, g1_ref, wg_ref, wb_ref, wo_ref, o_ref):
    x = jnp.where(pl.program_id(0) < CTX_ROWS // ROW_TM, xc_ref[...], xl_ref[...])
    mod = mod_ref[...]
    h = _rms_mod(x, g1_ref[...], mod[1:2], mod[0:1]).astype(BF16)
    merged = None
    for n in range(N_BRANCH):
        gate = jax.nn.sigmoid(_dot(h, wg_ref[:, n * D_MODEL:(n + 1) * D_MODEL]))
        term = gate * _dot(br_ref[:, n * BRANCH_W:(n + 1) * BRANCH_W], wb_ref[n])
        merged = term if merged is None else merged + term
    o_ref[...] = x + mod[2:3] * _dot(merged.astype(BF16), wo_ref[...])


def _mlp_block(x, mod, g2, w1_ref, w2_ref):
    h = _rms_mod(x, g2, mod[4:5], mod[3:4]).astype(BF16)
    acc = None
    for n in range(D_FF // D_MODEL):
        cols = slice(n * D_MODEL, (n + 1) * D_MODEL)
        hid = jnp.maximum(_dot(h, w1_ref[:, cols]), 0.0)
        term = _dot((hid * hid).astype(BF16), w2_ref[cols, :])
        acc = term if acc is None else acc + term
    return x + mod[5:6] * acc


def _mlp_kernel(x_ref, mod_ref, g2_ref, w1_ref, w2_ref, o_ref):
    o_ref[...] = _mlp_block(x_ref[...], mod_ref[...], g2_ref[...], w1_ref, w2_ref)


def _mlp_final_kernel(x_ref, mod_ref, g2_ref, w1_ref, w2_ref, gf_ref, yc_ref, yl_ref):
    x = _mlp_block(x_ref[...], mod_ref[...], g2_ref[...], w1_ref, w2_ref)
    y = x * lax.rsqrt(jnp.mean(x * x, axis=-1, keepdims=True) + EPS) * gf_ref[...]
    is_ctx = pl.program_id(0) < CTX_ROWS // ROW_TM

    @pl.when(is_ctx)
    def _():
        yc_ref[...] = y

    @pl.when(jnp.logical_not(is_ctx))
    def _():
        yl_ref[...] = y


def _params(*sem):
    return pltpu.CompilerParams(dimension_semantics=sem, vmem_limit_bytes=VMEM_LIMIT)


def _full(shape):
    nd = len(shape)
    return pl.BlockSpec(shape, lambda *_: (0,) * nd)


def _layer_block(shape, layer):
    nd = len(shape)
    return pl.BlockSpec((None,) + tuple(shape), lambda *_: (layer,) + (0,) * nd)


def _row_mod_index(layer, tm):
    ctx_tiles = CTX_ROWS // tm
    tiles_per_seq = DEC_SEQ // tm

    def index(i):
        sel = jnp.where(i < ctx_tiles, 0, 1 + (i - ctx_tiles) // tiles_per_seq)
        return (layer, sel, 0, 0)
    return index


def _ada_call(cond, w_ada, b_ada):
    return pl.pallas_call(
        _ada_kernel,
        grid=(DEPTH, N_MOD),
        in_specs=[
            pl.BlockSpec((COND_ROWS, D_MODEL), lambda l, j: (0, 0)),
            pl.BlockSpec((None, D_MODEL, D_MODEL), lambda l, j: (l, 0, j)),
            pl.BlockSpec((None, 1, D_MODEL), lambda l, j: (l, 0, j)),
        ],
        out_specs=pl.BlockSpec((None, COND_ROWS, D_MODEL), lambda l, j: (l, 0, j)),
        out_shape=jax.ShapeDtypeStruct((DEPTH, COND_ROWS, N_MOD * D_MODEL), F32),
        compiler_params=_params("arbitrary", "arbitrary"),
        name="ada",
    )(cond, w_ada, b_ada.reshape(DEPTH, 1, N_MOD * D_MODEL))


def _ctx_call(layer, x, mods, p, caches):
    ns = CTX_SEQS_PER_STEP
    tm = ns * SEQ
    cache_shapes = [(BATCH, DEPTH, SEQ, 128), (BATCH, DEPTH, SEQ, 128),
                    (BATCH, DEPTH, SEQ, 256), (BATCH, DEPTH, SEQ, 256)]
    in_specs = [
        pl.BlockSpec((tm, D_MODEL), lambda i: (i, 0)),
        pl.BlockSpec((None, None, N_MOD, D_MODEL), lambda i: (layer, 0, 0, 0)),
        _layer_block((1, D_MODEL), layer),
        _layer_block((D_MODEL, IN_W), layer),
        _layer_block((1, 256), layer),
        _layer_block((1, 128), layer),
        _layer_block((1, 256), layer),
        _layer_block((1, 256), layer),
        _layer_block((4, DIFF_QK_DIM), layer),
        _full((SEQ, 2 * SEQ)),
        _full((BRANCH_W, BRANCH_W)),
        _full((BRANCH_W, BRANCH_W)),
        _layer_block((BRANCH_W, BRANCH_W), layer),
        _layer_block((SGU_GROUPS * SGU_CHUNK, SGU_CHUNK), layer),
        _layer_block((SGU_CHUNK, BRANCH_W), layer),
    ]
    args = [x, mods, p["norm1_g"], p["w_in"], p["gq"], p["gk"], p["gsgu"], p["gdiff"], p["lamv"],
            p["dft_ctx"], p["bdc"], p["bds"], p["w_fourier"], p["wsp"], p["bsp"]]
    aliases = {}
    if caches is not None:
        for j, cache in enumerate(caches):
            in_specs.append(pl.BlockSpec(memory_space=pl.ANY))
            aliases[len(args)] = 1 + j
            args.append(cache)
    kernel = functools.partial(_ctx_kernel, layer)
    if caches is not None:
        base = kernel

        def kernel(*refs):
            n_in = 15
            return base(*refs[:n_in], *refs[n_in + 4:])
    out_specs = [pl.BlockSpec((tm, D_MODEL), lambda i: (i, 0))] + [
        pl.BlockSpec((ns, None, SEQ, s[-1]), lambda i: (i, layer, 0, 0)) for s in cache_shapes]
    out_shape = [jax.ShapeDtypeStruct((ALL_ROWS, D_MODEL), BF16)] + [
        jax.ShapeDtypeStruct(s, F32) for s in cache_shapes]
    outs = pl.pallas_call(
        kernel,
        grid=(BATCH // ns,),
        in_specs=in_specs,
        out_specs=out_specs,
        out_shape=out_shape,
        input_output_aliases=aliases,
        compiler_params=_params("arbitrary"),
        name=f"ctx_mix_{layer}",
    )(*args)
    return outs[0], tuple(outs[1:])


def _lat_pre_call(layer, x, row0, mods, p):
    tm = LAT_PRE_TM
    tile0 = row0 // tm
    tiles_per_seq = DEC_SEQ // tm
    rope_spec = pl.BlockSpec((tm, LANES), lambda i: (i % tiles_per_seq, 0))
    rows = lambda w, dt: (pl.BlockSpec((tm, w), lambda i: (i, 0)),
                          jax.ShapeDtypeStruct((LAT_ROWS, w), dt))
    cols = lambda w: (pl.BlockSpec((None, w, tm), lambda i: (i // tiles_per_seq, 0, i % tiles_per_seq)),
                      jax.ShapeDtypeStruct((DEC_BATCH, w, DEC_SEQ), BF16))
    outs = [rows(256, BF16), rows(256, BF16), rows(128, BF16), cols(128),
            rows(256, BF16), rows(256, BF16), cols(256), rows(256, F32), rows(256, BF16)]
    return pl.pallas_call(
        _lat_pre_kernel,
        grid=(LAT_ROWS // tm,),
        in_specs=[
            pl.BlockSpec((tm, D_MODEL), lambda i: (tile0 + i, 0)),
            pl.BlockSpec((None, None, N_MOD, D_MODEL), lambda i: (layer, 1 + i // tiles_per_seq, 0, 0)),
            _layer_block((1, D_MODEL), layer),
            _layer_block((D_MODEL, IN_W), layer),
            _layer_block((1, 256), layer),
            _layer_block((1, 128), layer),
            _layer_block((1, 256), layer),
            rope_spec, rope_spec, rope_spec, rope_spec,
        ],
        out_specs=[spec for spec, _ in outs],
        out_shape=[shape for _, shape in outs],
        compiler_params=_params("arbitrary"),
        name=f"lat_pre_{layer}",
    )(x, mods, p["norm1_g"], p["w_in"], p["gq"], p["gk"], p["gsgu"],
      p["rope_cb"], p["rope_sb"], p["rope_cc"], p["rope_sc"])


def _lat_mix_call(layer, br, pre, caches, p):
    tq = LAT_TQ
    nq = DEC_SEQ // tq
    ctx_tiles = CTX_ROWS // tq
    seq_block = lambda w: pl.BlockSpec((DEC_SEQ, w), lambda b, q: (b, 0))
    seq_block_t = lambda w: pl.BlockSpec((None, w, DEC_SEQ), lambda b, q: (b, 0, 0))
    q_block = lambda w: pl.BlockSpec((tq, w), lambda b, q: (b * nq + q, 0))
    cache_block = lambda w: pl.BlockSpec((None, None, PAST_LEN, w), lambda b, q: (b, layer, 0, 0))
    return pl.pallas_call(
        functools.partial(_lat_mix_kernel, layer),
        grid=(DEC_BATCH, nq),
        in_specs=[
            pl.BlockSpec(memory_space=pl.ANY),
            seq_block(256), q_block(256), seq_block(128), seq_block_t(128),
            q_block(256), seq_block(256), seq_block_t(256),
            q_block(256), q_block(256),
            cache_block(128), cache_block(128), cache_block(256), cache_block(256),
            _layer_block((1, 256), layer),
            _layer_block((4, DIFF_QK_DIM), layer),
            pl.BlockSpec((tq, 2 * DEC_SEQ), lambda b, q: (q, 0)),
            _full((BRANCH_W, BRANCH_W)),
            _full((BRANCH_W, BRANCH_W)),
            _layer_block((BRANCH_W, BRANCH_W), layer),
            _layer_block((SGU_GROUPS * SGU_CHUNK, SGU_CHUNK), layer),
            _layer_block((SGU_CHUNK, BRANCH_W), layer),
        ],
        out_specs=pl.BlockSpec((tq, D_MODEL), lambda b, q: (ctx_tiles + b * nq + q, 0)),
        out_shape=jax.ShapeDtypeStruct((ALL_ROWS, D_MODEL), BF16),
        scratch_shapes=[
            pltpu.VMEM((KEYS_LAT, 128), BF16), pltpu.VMEM((128, KEYS_LAT), BF16),
            pltpu.VMEM((KEYS_LAT, 256), BF16), pltpu.VMEM((256, KEYS_LAT), BF16),
            pltpu.VMEM((2 * DEC_SEQ, BRANCH_W), BF16),
        ],
        input_output_aliases={0: 0},
        compiler_params=_params("arbitrary", "arbitrary"),
        name=f"lat_mix_{layer}",
    )(br, *pre, *caches, p["gdiff"], p["lamv"],
      p["dft_lat"], p["bdc"], p["bds"], p["w_fourier"], p["wsp"], p["bsp"])


def _merge_call(layer, x_ctx, x_lat, lat_row0, br, mods, p):
    tm = ROW_TM
    ctx_tiles = CTX_ROWS // tm
    lat_tile0 = lat_row0 // tm
    return pl.pallas_call(
        _merge_kernel,
        grid=(ALL_ROWS // tm,),
        in_specs=[
            pl.BlockSpec((tm, D_MODEL), lambda i: (jnp.minimum(i, ctx_tiles - 1), 0)),
            pl.BlockSpec((tm, D_MODEL), lambda i: (lat_tile0 + jnp.maximum(i - ctx_tiles, 0), 0)),
            pl.BlockSpec((tm, D_MODEL), lambda i: (i, 0)),
            pl.BlockSpec((None, None, N_MOD, D_MODEL), _row_mod_index(layer, tm)),
            _layer_block((1, D_MODEL), layer),
            _layer_block((D_MODEL, N_BRANCH * D_MODEL), layer),
            _layer_block((N_BRANCH, BRANCH_W, D_MODEL), layer),
            _layer_block((D_MODEL, D_MODEL), layer),
        ],
        out_specs=pl.BlockSpec((tm, D_MODEL), lambda i: (i, 0)),
        out_shape=jax.ShapeDtypeStruct((ALL_ROWS, D_MODEL), F32),
        compiler_params=_params("arbitrary"),
        name=f"merge_{layer}",
    )(x_ctx, x_lat, br, mods, p["norm1_g"], p["w_gate"], p["w_branch"], p["w_out"])


def _mlp_call(layer, x, mods, p, final_g=None):
    tm = ROW_TM
    ctx_tiles = CTX_ROWS // tm
    in_specs = [
        pl.BlockSpec((tm, D_MODEL), lambda i: (i, 0)),
        pl.BlockSpec((None, None, N_MOD, D_MODEL), _row_mod_index(layer, tm)),
        _layer_block((1, D_MODEL), layer),
        _layer_block((D_MODEL, D_FF), layer),
        _layer_block((D_FF, D_MODEL), layer),
    ]
    args = [x, mods, p["norm2_g"], p["w_mlp1"], p["w_mlp2"]]
    if final_g is None:
        body = _mlp_kernel
        out_specs = pl.BlockSpec((tm, D_MODEL), lambda i: (i, 0))
        out_shape = jax.ShapeDtypeStruct((ALL_ROWS, D_MODEL), F32)
    else:
        body = _mlp_final_kernel
        in_specs.append(_full((1, D_MODEL)))
        args.append(final_g)
        out_specs = [
            pl.BlockSpec((tm, D_MODEL), lambda i: (jnp.minimum(i, ctx_tiles - 1), 0)),
            pl.BlockSpec((tm, D_MODEL), lambda i: (jnp.maximum(i - ctx_tiles, 0), 0)),
        ]
        out_shape = [jax.ShapeDtypeStruct((CTX_ROWS, D_MODEL), F32),
                     jax.ShapeDtypeStruct((LAT_ROWS, D_MODEL), F32)]
    return pl.pallas_call(
        body,
        grid=(ALL_ROWS // tm,),
        in_specs=in_specs,
        out_specs=out_specs,
        out_shape=out_shape,
        compiler_params=_params("arbitrary"),
        name=f"mlp_{layer}",
    )(*args)


def _permute_q_heads(w, axis):
    shape = w.shape
    w = w.reshape(shape[:axis] + (GQA_HEADS, GQA_HEAD_DIM) + shape[axis + 1:])
    w = jnp.take(w, jnp.array(_Q_HEAD_ORDER), axis=axis)
    return w.reshape(shape)


def kernel(x_prompt, x_sample, c, cache_gqa_k, cache_gqa_v, cache_diff_k, cache_diff_v, c_ctx, w_ada, b_ada, norm1_g, norm2_g, w_in, w_fourier, q_norm_g, k_norm_g, lambda_q1, lambda_k1, lambda_q2, lambda_k2, diff_norm_g, sgu_norm_g, w_spatial, b_spatial, w_gate, w_branch, w_out, w_mlp1, w_mlp2, final_norm_g):
    w_in_p = jnp.concatenate(
        [w_in[..., :_QB0], _permute_q_heads(w_in[..., _QB0:_KB0], 2), w_in[..., _KB0:]], axis=-1)
    bdc, bds = _channel_dft_tables()
    cb, sb = _rope_tables(DEC_SEQ, GQA_HEAD_DIM)
    cc, sc = _rope_tables(DEC_SEQ, DIFF_QK_DIM)
    p = {
        "norm1_g": norm1_g.reshape(DEPTH, 1, D_MODEL),
        "norm2_g": norm2_g.reshape(DEPTH, 1, D_MODEL),
        "w_in": w_in_p.astype(BF16),
        "w_fourier": w_fourier.astype(BF16),
        "gq": jnp.tile(q_norm_g, (1, GQA_HEADS)).reshape(DEPTH, 1, 256),
        "gk": jnp.tile(k_norm_g, (1, GQA_KV_HEADS)).reshape(DEPTH, 1, 128),
        "gsgu": sgu_norm_g.reshape(DEPTH, 1, BRANCH_W),
        "gdiff": jnp.tile(diff_norm_g, (1, DIFF_HEADS)).reshape(DEPTH, 1, 256),
        "lamv": jnp.stack([lambda_q1, lambda_k1, lambda_q2, lambda_k2], axis=1),
        "wsp": w_spatial.reshape(DEPTH, SGU_GROUPS * SGU_CHUNK, SGU_CHUNK).astype(BF16),
        "bsp": jnp.repeat(jnp.swapaxes(b_spatial, 1, 2), BRANCH_W // SGU_GROUPS, axis=2),
        "w_gate": w_gate.astype(BF16),
        "w_branch": w_branch.astype(BF16),
        "w_out": w_out.astype(BF16),
        "w_mlp1": w_mlp1.astype(BF16),
        "w_mlp2": w_mlp2.astype(BF16),
        "dft_ctx": jnp.asarray(_dft_tables(SEQ)).astype(BF16),
        "dft_lat": jnp.asarray(_dft_tables(DEC_SEQ)).astype(BF16),
        "bdc": jnp.asarray(bdc).astype(BF16),
        "bds": jnp.asarray(bds).astype(BF16),
        "rope_cb": jnp.asarray(cb), "rope_sb": jnp.asarray(sb),
        "rope_cc": jnp.asarray(cc), "rope_sc": jnp.asarray(sc),
    }
    lat_caches = (
        cache_gqa_k.reshape(DEC_BATCH, DEPTH, PAST_LEN, 128),
        cache_gqa_v.reshape(DEC_BATCH, DEPTH, PAST_LEN, 128),
        cache_diff_k.reshape(DEC_BATCH, DEPTH, PAST_LEN, 256),
        cache_diff_v.reshape(DEC_BATCH, DEPTH, PAST_LEN, 256),
    )

    cond = jnp.concatenate(
        [c_ctx[None, :], c, jnp.zeros((COND_ROWS - 1 - DEC_BATCH, D_MODEL), F32)], axis=0)
    mods = _ada_call(cond, w_ada, b_ada)[:, :1 + DEC_BATCH].reshape(
        DEPTH, 1 + DEC_BATCH, N_MOD, D_MODEL)

    x_ctx, x_lat, lat_row0 = x_prompt.reshape(CTX_ROWS, D_MODEL), x_sample.reshape(LAT_ROWS, D_MODEL), 0
    new_caches = None
    for layer in range(DEPTH):
        br, new_caches = _ctx_call(layer, x_ctx, mods, p, new_caches)
        pre = _lat_pre_call(layer, x_lat, lat_row0, mods, p)
        br = _lat_mix_call(layer, br, pre, lat_caches, p)
        x = _merge_call(layer, x_ctx, x_lat, lat_row0, br, mods, p)
        if layer + 1 < DEPTH:
            x = _mlp_call(layer, x, mods, p)
            x_ctx, x_lat, lat_row0 = x, x, CTX_ROWS
    y_ctx, y_lat = _mlp_call(DEPTH - 1, x, mods, p, final_g=final_norm_g.reshape(1, D_MODEL))

    gk, gv, dk, dv = new_caches
    return (
        y_ctx.reshape(BATCH, SEQ, D_MODEL),
        y_lat.reshape(DEC_BATCH, DEC_SEQ, D_MODEL),
        gk.reshape(BATCH, DEPTH, SEQ, GQA_KV_HEADS, GQA_HEAD_DIM),
        gv.reshape(BATCH, DEPTH, SEQ, GQA_KV_HEADS, GQA_HEAD_DIM),
        dk.reshape(BATCH, DEPTH, SEQ, DIFF_HEADS, 2, DIFF_QK_DIM),
        dv.reshape(BATCH, DEPTH, SEQ, DIFF_HEADS, DIFF_V_DIM),
    )
```

```python
import functools
import math

import numpy as np
import jax
import jax.numpy as jnp
from jax import lax
from jax.experimental import pallas as pl
from jax.experimental.pallas import tpu as pltpu

F32 = jnp.float32
BF16 = jnp.bfloat16

D_MODEL = 1024
BATCH = 32
SEQ = 256
DEPTH = 4
DEC_BATCH = 2
DEC_SEQ = 2048
PAST_LEN = 512
GRID_W = 64
ROPE_THETA = 10000.0
EPS = 1e-6
N_BRANCH = 4
BRANCH_W = 256
FNET_GW = 64
GQA_HEADS = 4
GQA_KV_HEADS = 2
GQA_HEAD_DIM = 64
DIFF_HEADS = 4
DIFF_V_DIM = 64
DIFF_QK_DIM = 32
SGU_GROUPS = 4
SGU_CHUNK = 128
D_FF = 4 * D_MODEL
N_MOD = 6
IN_W = 2048

CTX_ROWS = BATCH * SEQ
LAT_ROWS = DEC_BATCH * DEC_SEQ
ALL_ROWS = CTX_ROWS + LAT_ROWS
KEYS_LAT = PAST_LEN + DEC_SEQ

LANES = 128
VMEM_LIMIT = 52 * 1024 * 1024

CTX_SEQS_PER_STEP = 4
LAT_PRE_TM = 512
LAT_TQ = 256
KEY_CHUNK = 256
ROW_TM = 512
COND_ROWS = 8

_A0, _QB0, _KB0, _VB0, _QC0, _KC0, _VC0, _DU0, _DV0 = 0, 256, 512, 640, 768, 1024, 1280, 1536, 1792
_Q_HEAD_ORDER = (0, 2, 1, 3)
_LOG2E = math.log2(math.e)
_GQA_Q_SCALE = _LOG2E / math.sqrt(GQA_HEAD_DIM)
_DIFF_Q_SCALE = _LOG2E / math.sqrt(DIFF_QK_DIM)


def _dft_tables(length):
    k = np.arange(length, dtype=np.int64)
    idx = (k[:, None] * k[None, :]) % length
    ang = 2.0 * np.pi * idx.astype(np.float64) / length
    return np.concatenate([np.cos(ang), -np.sin(ang)], axis=1).astype(np.float32)


def _channel_dft_tables():
    k = np.arange(FNET_GW, dtype=np.int64)
    idx = (k[:, None] * k[None, :]) % FNET_GW
    ang = 2.0 * np.pi * idx.astype(np.float64) / FNET_GW
    c = np.cos(ang) / math.sqrt(FNET_GW)
    s = np.sin(ang) / math.sqrt(FNET_GW)
    eye = np.eye(BRANCH_W // FNET_GW)
    return (np.kron(eye, c).astype(np.float32), np.kron(eye, s).astype(np.float32))


def _rope_tables(length, dim):
    rows = length // GRID_W
    row = np.repeat(np.arange(rows, dtype=np.float64), GRID_W)
    col = np.tile(np.arange(GRID_W, dtype=np.float64), rows)
    quarter = dim // 4
    inv = ROPE_THETA ** (-np.arange(quarter, dtype=np.float64) / quarter)
    ang = np.concatenate([row[:, None] * inv, col[:, None] * inv], axis=-1)
    c = np.concatenate([np.cos(ang), np.cos(ang)], axis=-1)
    s = np.concatenate([-np.sin(ang), np.sin(ang)], axis=-1)
    reps = LANES // dim
    return (np.tile(c, (1, reps)).astype(np.float32), np.tile(s, (1, reps)).astype(np.float32))


def _dot(a, b):
    return jnp.dot(a, b, preferred_element_type=F32)


def _dot_nt(a, b):
    return lax.dot_general(a, b, (((1,), (1,)), ((), ())), preferred_element_type=F32)


def _rms_mod(x, g, scale, shift):
    y = x * lax.rsqrt(jnp.mean(x * x, axis=-1, keepdims=True) + EPS) * g
    return y * (1.0 + scale) + shift


def _gelu(x):
    c = math.sqrt(2.0 / math.pi)
    return 0.5 * x * (1.0 + jnp.tanh(c * (x + 0.044715 * (x * x * x))))


def _group_ones(width, group):
    r = lax.broadcasted_iota(jnp.int32, (width, width), 0) // group
    c = lax.broadcasted_iota(jnp.int32, (width, width), 1) // group
    return (r == c).astype(BF16)


def _group_mean_sq(x, group):
    x2 = x * x
    hi = x2.astype(BF16)
    lo = (x2 - hi.astype(F32)).astype(BF16)
    ones = _group_ones(x.shape[-1], group)
    return (_dot(hi, ones) + _dot(lo, ones)) * (1.0 / group)


def _lane_mask(lo, hi):
    lane = lax.broadcasted_iota(jnp.int32, (1, LANES), 1)
    return (lane >= lo) & (lane < hi)


def _swap_halves(x, dim):
    n = x.shape[-1]
    half = dim // 2
    lane = lax.broadcasted_iota(jnp.int32, (1, n), 1)
    fwd = pltpu.roll(x, n - half, 1)
    bwd = pltpu.roll(x, half, 1)
    return jnp.where((lane % dim) < half, fwd, bwd)


def _rope(x, cos, sin, dim):
    reps = x.shape[-1] // LANES
    if reps > 1:
        cos = jnp.concatenate([cos] * reps, axis=1)
        sin = jnp.concatenate([sin] * reps, axis=1)
    return x * cos + _swap_halves(x, dim) * sin


def _fold8(x, op):
    return op(x.reshape(x.shape[0] // 8, 8, x.shape[1]), axis=0)


class _Unit:
    def __init__(self, k_ref, k_lanes, q_blk, masks, vt_ref, v_rows, out_row0, is_diff):
        self.k_ref, self.k_lanes, self.vt_ref, self.v_rows = k_ref, k_lanes, vt_ref, v_rows
        self.out_row0, self.is_diff = out_row0, is_diff
        self.qm = [jnp.where(_lane_mask(lo, hi), q_blk, jnp.zeros_like(q_blk)) for lo, hi in masks]
        self.m = self.scale = None


def _make_units(qb, qc, kb_ref, kc_ref, vbt_ref, vct_ref):
    gqa_units, diff_units = [], []
    for pos, head in enumerate(_Q_HEAD_ORDER):
        blk, kv = pos // 2, pos % 2
        rows = slice(kv * GQA_HEAD_DIM, (kv + 1) * GQA_HEAD_DIM)
        gqa_units.append(_Unit(kb_ref, slice(0, LANES), qb[:, blk * LANES:(blk + 1) * LANES],
                               [(rows.start, rows.stop)], vbt_ref, rows, head * GQA_HEAD_DIM, False))
    for head in range(DIFF_HEADS):
        blk, base = head // 2, (head % 2) * DIFF_V_DIM
        lanes = slice(blk * LANES, (blk + 1) * LANES)
        diff_units.append(_Unit(kc_ref, lanes, qc[:, lanes],
                                [(base, base + DIFF_QK_DIM), (base + DIFF_QK_DIM, base + DIFF_V_DIM)],
                                vct_ref, slice(head * DIFF_V_DIM, (head + 1) * DIFF_V_DIM),
                                BRANCH_W + head * DIFF_V_DIM, True))
    return [u for pair in zip(gqa_units, diff_units) for u in pair]


def _attention_pipeline(units, lam, s_buf, ot_ref):
    n = len(units)
    tq = s_buf.shape[-1]
    for j in range(n + 2):
        ua = units[j] if j < n else None
        ub = units[j - 1] if 1 <= j <= n else None
        uc = units[j - 2] if 2 <= j <= n + 1 and units[j - 2].is_diff else None
        sa, sb, sc = j % 3, (j - 1) % 3, (j - 2) % 3
        init = {}
        if ua is not None:
            init["mx"] = [jnp.full((8, tq), -jnp.inf, F32) for _ in ua.qm]
        if ub is not None:
            init["l"] = [jnp.zeros((8, tq), F32) for _ in ub.qm]
            if not ub.is_diff:
                init["ob"] = jnp.zeros((DIFF_V_DIM, tq), F32)
        if uc is not None:
            init["oc"] = jnp.zeros((DIFF_V_DIM, tq), F32)

        def body(c, carry, ua=ua, ub=ub, uc=uc, sa=sa, sb=sb, sc=sc):
            new = dict(carry)
            if ua is not None:
                k_c = ua.k_ref[c, :, ua.k_lanes]
                mx = []
                for mi, qm in enumerate(ua.qm):
                    s = _dot_nt(k_c, qm)
                    s_buf[sa, mi, c] = s
                    mx.append(jnp.maximum(carry["mx"][mi], _fold8(s, jnp.max)))
                new["mx"] = mx
            if ub is not None:
                ls = []
                for mi in range(len(ub.qm)):
                    e = jnp.exp2(s_buf[sb, mi, c] - ub.m[mi])
                    ls.append(carry["l"][mi] + _fold8(e, jnp.sum))
                    if ub.is_diff:
                        s_buf[sb, mi, c] = e
                    else:
                        new["ob"] = carry["ob"] + _dot(ub.vt_ref[c, ub.v_rows, :], e.astype(BF16))
                new["l"] = ls
            if uc is not None:
                w = s_buf[sc, 0, c] * uc.scale[0] - s_buf[sc, 1, c] * uc.scale[1]
                new["oc"] = carry["oc"] + _dot(uc.vt_ref[c, uc.v_rows, :], w.astype(BF16))
            return new

        out = lax.fori_loop(0, s_buf.shape[2], body, init, unroll=True)
        if ua is not None:
            ua.m = [jnp.max(mx, axis=0, keepdims=True) for mx in out["mx"]]
        if ub is not None:
            l = [jnp.sum(x, axis=0, keepdims=True) for x in out["l"]]
            if ub.is_diff:
                ub.scale = [1.0 / l[0], lam / l[1]]
            else:
                ot_ref[ub.out_row0:ub.out_row0 + DIFF_V_DIM, :] = out["ob"] * (1.0 / l[0])
        if uc is not None:
            o = out["oc"]
            ot_ref[uc.out_row0:uc.out_row0 + DIFF_V_DIM, :] = (
                o * lax.rsqrt(jnp.mean(o * o, axis=0, keepdims=True) + EPS))


def _lambda(lamv, layer):
    lam_init = 0.8 - 0.6 * math.exp(-0.3 * layer)
    a = jnp.sum(lamv[0:1] * lamv[1:2], axis=-1, keepdims=True)
    b = jnp.sum(lamv[2:3] * lamv[3:4], axis=-1, keepdims=True)
    return jnp.exp(a) - jnp.exp(b) + lam_init, 1.0 - lam_init


def _sgu(du, dvn, wsp, bias):
    lane = lax.broadcasted_iota(jnp.int32, (1, BRANCH_W), 1)
    outs = []
    for n in range(du.shape[0] // SGU_CHUNK):
        rows = slice(n * SGU_CHUNK, (n + 1) * SGU_CHUNK)
        r = _dot(wsp, dvn[rows].astype(BF16))
        s = bias
        for g in range(SGU_GROUPS):
            gm = (lane >= g * 64) & (lane < (g + 1) * 64)
            s = s + jnp.where(gm, r[g * SGU_CHUNK:(g + 1) * SGU_CHUNK], 0.0)
        outs.append(du[rows] * s)
    return outs[0] if len(outs) == 1 else jnp.concatenate(outs, axis=0)


def _fourier_stage1(a, bdc, bds):
    return jnp.concatenate([_dot(a, bdc), _dot(a, bds)], axis=0).astype(BF16)


def _fourier_stage2(dft_rows, f, length, w_fourier):
    y = _dot(dft_rows, f) * (1.0 / math.sqrt(length))
    return _dot(y.astype(BF16), w_fourier)


def _ada_kernel(cond_ref, w_ref, b_ref, o_ref):
    c = cond_ref[...]
    s = c * jax.nn.sigmoid(c)
    o_ref[...] = _dot(s.astype(BF16), w_ref[...].astype(BF16)) + b_ref[...]


def _ctx_kernel(layer, x_ref, mod_ref, g1_ref, win_ref, gq_ref, gk_ref, gsgu_ref, gdiff_ref,
                lamv_ref, dft_ref, bdc_ref, bds_ref, wf_ref, wsp_ref, bsp_ref,
                br_ref, ok_ref, ov_ref, odk_ref, odv_ref,
                kb_s, vbt_s, kc_s, vct_s, s_buf, ot_s):
    x = x_ref[...]
    mod = mod_ref[...]
    h = _rms_mod(x, g1_ref[...], mod[1:2], mod[0:1]).astype(BF16)
    proj = _dot(h, win_ref[...])
    qb = proj[:, _QB0:_KB0]
    kb = proj[:, _KB0:_VB0]
    qb = qb * lax.rsqrt(_group_mean_sq(qb, GQA_HEAD_DIM) + EPS) * gq_ref[...]
    kb = kb * lax.rsqrt(_group_mean_sq(kb, GQA_HEAD_DIM) + EPS) * gk_ref[...]
    vb = proj[:, _VB0:_QC0]
    qc = proj[:, _QC0:_KC0]
    kc = proj[:, _KC0:_VC0]
    vc = proj[:, _VC0:_DU0]
    du = _gelu(proj[:, _DU0:_DV0])
    dv = _gelu(proj[:, _DV0:IN_W])
    dvn = dv * lax.rsqrt(jnp.mean(dv * dv, axis=-1, keepdims=True) + EPS) * gsgu_ref[...]
    lam, lam_scale = _lambda(lamv_ref[...], layer)
    qb = (qb * _GQA_Q_SCALE).astype(BF16)
    qc = (qc * _DIFF_Q_SCALE).astype(BF16)
    a = proj[:, _A0:_QB0].astype(BF16)
    for s in range(CTX_SEQS_PER_STEP):
        rows = slice(s * SEQ, (s + 1) * SEQ)
        ok_ref[s] = kb[rows]
        ov_ref[s] = vb[rows]
        odk_ref[s] = kc[rows]
        odv_ref[s] = vc[rows]
        kb_s[s, 0] = kb[rows].astype(BF16)
        kc_s[s, 0] = kc[rows].astype(BF16)
        vbt_s[s, 0] = vb[rows].T.astype(BF16)
        vct_s[s, 0] = vc[rows].T.astype(BF16)
        f = _fourier_stage1(a[rows], bdc_ref[...], bds_ref[...])
        br_ref[rows, 0:256] = _fourier_stage2(dft_ref[...], f, SEQ, wf_ref[...]).astype(BF16)
        br_ref[rows, 768:1024] = _sgu(du[rows], dvn[rows], wsp_ref[...], bsp_ref[...]).astype(BF16)
    for s in range(CTX_SEQS_PER_STEP):
        rows = slice(s * SEQ, (s + 1) * SEQ)
        units = _make_units(qb[rows], qc[rows], kb_s.at[s], kc_s.at[s], vbt_s.at[s], vct_s.at[s])
        _attention_pipeline(units, lam, s_buf.at[s], ot_s.at[s])
        br_ref[rows, 256:512] = ot_s[s, 0:BRANCH_W, :].T.astype(BF16)
        br_ref[rows, 512:768] = (ot_s[s, BRANCH_W:2 * BRANCH_W, :].T
                                 * (gdiff_ref[...] * lam_scale)).astype(BF16)


def _lat_pre_kernel(x_ref, mod_ref, g1_ref, win_ref, gq_ref, gk_ref, gsgu_ref,
                    cb_ref, sb_ref, cc_ref, sc_ref,
                    a_ref, qb_ref, kb_ref, vbt_ref, qc_ref, kc_ref, vct_ref, du_ref, dvn_ref):
    x = x_ref[...]
    mod = mod_ref[...]
    h = _rms_mod(x, g1_ref[...], mod[1:2], mod[0:1]).astype(BF16)
    proj = _dot(h, win_ref[...])
    cb, sb, cc, sc = cb_ref[...], sb_ref[...], cc_ref[...], sc_ref[...]
    qb = proj[:, _QB0:_KB0]
    kb = proj[:, _KB0:_VB0]
    qb = qb * lax.rsqrt(_group_mean_sq(qb, GQA_HEAD_DIM) + EPS) * gq_ref[...]
    kb = kb * lax.rsqrt(_group_mean_sq(kb, GQA_HEAD_DIM) + EPS) * gk_ref[...]
    qb = _rope(qb, cb, sb, GQA_HEAD_DIM) * _GQA_Q_SCALE
    kb = _rope(kb, cb, sb, GQA_HEAD_DIM)
    qc = _rope(proj[:, _QC0:_KC0], cc, sc, DIFF_QK_DIM) * _DIFF_Q_SCALE
    kc = _rope(proj[:, _KC0:_VC0], cc, sc, DIFF_QK_DIM)
    dv = _gelu(proj[:, _DV0:IN_W])
    a_ref[...] = proj[:, _A0:_QB0].astype(BF16)
    qb_ref[...] = qb.astype(BF16)
    kb_ref[...] = kb.astype(BF16)
    vbt_ref[...] = proj[:, _VB0:_QC0].T.astype(BF16)
    qc_ref[...] = qc.astype(BF16)
    kc_ref[...] = kc.astype(BF16)
    vct_ref[...] = proj[:, _VC0:_DU0].T.astype(BF16)
    du_ref[...] = _gelu(proj[:, _DU0:_DV0])
    dvn_ref[...] = (dv * lax.rsqrt(jnp.mean(dv * dv, axis=-1, keepdims=True) + EPS)
                    * gsgu_ref[...]).astype(BF16)


def _lat_mix_kernel(layer, br_in_ref, a_ref, qb_ref, kb_ref, vbt_ref, qc_ref, kc_ref, vct_ref,
                    du_ref, dvn_ref, ck_ref, cv_ref, cdk_ref, cdv_ref, gdiff_ref, lamv_ref,
                    dft_ref, bdc_ref, bds_ref, wf_ref, wsp_ref, bsp_ref,
                    br_ref, kb_s, vbt_s, kc_s, vct_s, f_s, s_buf, ot_s):
    del br_in_ref
    kc = KEY_CHUNK
    past_chunks = PAST_LEN // kc

    @pl.when(pl.program_id(1) == 0)
    def _fill():
        for c in range(KEYS_LAT // kc):
            if c < past_chunks:
                rows = slice(c * kc, (c + 1) * kc)
                kb_s[c] = ck_ref[rows, :].astype(BF16)
                kc_s[c] = cdk_ref[rows, :].astype(BF16)
                vbt_s[c] = cv_ref[rows, :].T.astype(BF16)
                vct_s[c] = cdv_ref[rows, :].T.astype(BF16)
            else:
                rows = slice((c - past_chunks) * kc, (c - past_chunks + 1) * kc)
                kb_s[c] = kb_ref[rows, :]
                kc_s[c] = kc_ref[rows, :]
                vbt_s[c] = vbt_ref[:, rows]
                vct_s[c] = vct_ref[:, rows]
        f_s[...] = _fourier_stage1(a_ref[...], bdc_ref[...], bds_ref[...])

    lam, lam_scale = _lambda(lamv_ref[...], layer)
    br_ref[:, 0:256] = _fourier_stage2(dft_ref[...], f_s[...], DEC_SEQ, wf_ref[...]).astype(BF16)
    br_ref[:, 768:1024] = _sgu(du_ref[...], dvn_ref[...].astype(F32), wsp_ref[...],
                               bsp_ref[...]).astype(BF16)

    units = _make_units(qb_ref[...], qc_ref[...], kb_s, kc_s, vbt_s, vct_s)
    _attention_pipeline(units, lam, s_buf, ot_s)
    br_ref[:, 256:512] = ot_s[0:BRANCH_W, :].T.astype(BF16)
    br_ref[:, 512:768] = (ot_s[BRANCH_W:2 * BRANCH_W, :].T * (gdiff_ref[...] * lam_scale)).astype(BF16)


def _merge_kernel(xc_ref, xl_ref, br_ref, mod_ref, g1_ref, wg_ref, wb_ref, wo_ref, o_ref):
    x = jnp.where(pl.program_id(0) < CTX_ROWS // ROW_TM, xc_ref[...], xl_ref[...])
    mod = mod_ref[...]
    h = _rms_mod(x, g1_ref[...], mod[1:2], mod[0:1]).astype(BF16)
    merged = None
    for n in range(N_BRANCH):
        gate = jax.nn.sigmoid(_dot(h, wg_ref[:, n * D_MODEL:(n + 1) * D_MODEL]))
        term = gate * _dot(br_ref[:, n * BRANCH_W:(n + 1) * BRANCH_W], wb_ref[n])
        merged = term if merged is None else merged + term
    o_ref[...] = x + mod[2:3] * _dot(merged.astype(BF16), wo_ref[...])


def _mlp_block(x, mod, g2, w1_ref, w2_ref):
    h = _rms_mod(x, g2, mod[4:5], mod[3:4]).astype(BF16)
    acc = None
    for n in range(D_FF // D_MODEL):
        cols = slice(n * D_MODEL, (n + 1) * D_MODEL)
        hid = jnp.maximum(_dot(h, w1_ref[:, cols]), 0.0)
        term = _dot((hid * hid).astype(BF16), w2_ref[cols, :])
        acc = term if acc is None else acc + term
    return x + mod[5:6] * acc


def _mlp_kernel(x_ref, mod_ref, g2_ref, w1_ref, w2_ref, o_ref):
    o_ref[...] = _mlp_block(x_ref[...], mod_ref[...], g2_ref[...], w1_ref, w2_ref)


def _mlp_final_kernel(x_ref, mod_ref, g2_ref, w1_ref, w2_ref, gf_ref, yc_ref, yl_ref):
    x = _mlp_block(x_ref[...], mod_ref[...], g2_ref[...], w1_ref, w2_ref)
    y = x * lax.rsqrt(jnp.mean(x * x, axis=-1, keepdims=True) + EPS) * gf_ref[...]
    is_ctx = pl.program_id(0) < CTX_ROWS // ROW_TM

    @pl.when(is_ctx)
    def _():
        yc_ref[...] = y

    @pl.when(jnp.logical_not(is_ctx))
    def _():
        yl_ref[...] = y


def _params(*sem):
    return pltpu.CompilerParams(dimension_semantics=sem, vmem_limit_bytes=VMEM_LIMIT)


def _full(shape):
    nd = len(shape)
    return pl.BlockSpec(shape, lambda *_: (0,) * nd)


def _layer_block(shape, layer):
    nd = len(shape)
    return pl.BlockSpec((None,) + tuple(shape), lambda *_: (layer,) + (0,) * nd)


def _row_mod_index(layer, tm):
    ctx_tiles = CTX_ROWS // tm
    tiles_per_seq = DEC_SEQ // tm

    def index(i):
        sel = jnp.where(i < ctx_tiles, 0, 1 + (i - ctx_tiles) // tiles_per_seq)
        return (layer, sel, 0, 0)
    return index


def _ada_call(cond, w_ada, b_ada):
    return pl.pallas_call(
        _ada_kernel,
        grid=(DEPTH, N_MOD),
        in_specs=[
            pl.BlockSpec((COND_ROWS, D_MODEL), lambda l, j: (0, 0)),
            pl.BlockSpec((None, D_MODEL, D_MODEL), lambda l, j: (l, 0, j)),
            pl.BlockSpec((None, 1, D_MODEL), lambda l, j: (l, 0, j)),
        ],
        out_specs=pl.BlockSpec((None, COND_ROWS, D_MODEL), lambda l, j: (l, 0, j)),
        out_shape=jax.ShapeDtypeStruct((DEPTH, COND_ROWS, N_MOD * D_MODEL), F32),
        compiler_params=_params("arbitrary", "arbitrary"),
        name="ada",
    )(cond, w_ada, b_ada.reshape(DEPTH, 1, N_MOD * D_MODEL))


def _ctx_call(layer, x, mods, p, caches):
    ns = CTX_SEQS_PER_STEP
    tm = ns * SEQ
    cache_shapes = [(BATCH, DEPTH, SEQ, 128), (BATCH, DEPTH, SEQ, 128),
                    (BATCH, DEPTH, SEQ, 256), (BATCH, DEPTH, SEQ, 256)]
    in_specs = [
        pl.BlockSpec((tm, D_MODEL), lambda i: (i, 0)),
        pl.BlockSpec((None, None, N_MOD, D_MODEL), lambda i: (layer, 0, 0, 0)),
        _layer_block((1, D_MODEL), layer),
        _layer_block((D_MODEL, IN_W), layer),
        _layer_block((1, 256), layer),
        _layer_block((1, 128), layer),
        _layer_block((1, 256), layer),
        _layer_block((1, 256), layer),
        _layer_block((4, DIFF_QK_DIM), layer),
        _full((SEQ, 2 * SEQ)),
        _full((BRANCH_W, BRANCH_W)),
        _full((BRANCH_W, BRANCH_W)),
        _layer_block((BRANCH_W, BRANCH_W), layer),
        _layer_block((SGU_GROUPS * SGU_CHUNK, SGU_CHUNK), layer),
        _layer_block((SGU_CHUNK, BRANCH_W), layer),
    ]
    args = [x, mods, p["norm1_g"], p["w_in"], p["gq"], p["gk"], p["gsgu"], p["gdiff"], p["lamv"],
            p["dft_ctx"], p["bdc"], p["bds"], p["w_fourier"], p["wsp"], p["bsp"]]
    aliases = {}
    if caches is not None:
        for j, cache in enumerate(caches):
            in_specs.append(pl.BlockSpec(memory_space=pl.ANY))
            aliases[len(args)] = 1 + j
            args.append(cache)
    kernel = functools.partial(_ctx_kernel, layer)
    if caches is not None:
        base = kernel

        def kernel(*refs):
            n_in = 15
            return base(*refs[:n_in], *refs[n_in + 4:])
    out_specs = [pl.BlockSpec((tm, D_MODEL), lambda i: (i, 0))] + [
        pl.BlockSpec((ns, None, SEQ, s[-1]), lambda i: (i, layer, 0, 0)) for s in cache_shapes]
    out_shape = [jax.ShapeDtypeStruct((ALL_ROWS, D_MODEL), BF16)] + [
        jax.ShapeDtypeStruct(s, F32) for s in cache_shapes]
    outs = pl.pallas_call(
        kernel,
        grid=(BATCH // ns,),
        in_specs=in_specs,
        out_specs=out_specs,
        out_shape=out_shape,
        scratch_shapes=[
            pltpu.VMEM((ns, 1, SEQ, 128), BF16), pltpu.VMEM((ns, 1, 128, SEQ), BF16),
            pltpu.VMEM((ns, 1, SEQ, 256), BF16), pltpu.VMEM((ns, 1, 256, SEQ), BF16),
            pltpu.VMEM((ns, 3, 2, 1, SEQ, SEQ), F32),
            pltpu.VMEM((ns, 2 * BRANCH_W, SEQ), F32),
        ],
        input_output_aliases=aliases,
        compiler_params=_params("arbitrary"),
        name=f"ctx_mix_{layer}",
    )(*args)
    return outs[0], tuple(outs[1:])


def _lat_pre_call(layer, x, row0, mods, p):
    tm = LAT_PRE_TM
    tile0 = row0 // tm
    tiles_per_seq = DEC_SEQ // tm
    rope_spec = pl.BlockSpec((tm, LANES), lambda i: (i % tiles_per_seq, 0))
    rows = lambda w, dt: (pl.BlockSpec((tm, w), lambda i: (i, 0)),
                          jax.ShapeDtypeStruct((LAT_ROWS, w), dt))
    cols = lambda w: (pl.BlockSpec((None, w, tm), lambda i: (i // tiles_per_seq, 0, i % tiles_per_seq)),
                      jax.ShapeDtypeStruct((DEC_BATCH, w, DEC_SEQ), BF16))
    outs = [rows(256, BF16), rows(256, BF16), rows(128, BF16), cols(128),
            rows(256, BF16), rows(256, BF16), cols(256), rows(256, F32), rows(256, BF16)]
    return pl.pallas_call(
        _lat_pre_kernel,
        grid=(LAT_ROWS // tm,),
        in_specs=[
            pl.BlockSpec((tm, D_MODEL), lambda i: (tile0 + i, 0)),
            pl.BlockSpec((None, None, N_MOD, D_MODEL), lambda i: (layer, 1 + i // tiles_per_seq, 0, 0)),
            _layer_block((1, D_MODEL), layer),
            _layer_block((D_MODEL, IN_W), layer),
            _layer_block((1, 256), layer),
            _layer_block((1, 128), layer),
            _layer_block((1, 256), layer),
            rope_spec, rope_spec, rope_spec, rope_spec,
        ],
        out_specs=[spec for spec, _ in outs],
        out_shape=[shape for _, shape in outs],
        compiler_params=_params("arbitrary"),
        name=f"lat_pre_{layer}",
    )(x, mods, p["norm1_g"], p["w_in"], p["gq"], p["gk"], p["gsgu"],
      p["rope_cb"], p["rope_sb"], p["rope_cc"], p["rope_sc"])


def _lat_mix_call(layer, br, pre, caches, p):
    tq = LAT_TQ
    nq = DEC_SEQ // tq
    ctx_tiles = CTX_ROWS // tq
    n_chunks = KEYS_LAT // KEY_CHUNK
    seq_block = lambda w: pl.BlockSpec((DEC_SEQ, w), lambda b, q: (b, 0))
    seq_block_t = lambda w: pl.BlockSpec((None, w, DEC_SEQ), lambda b, q: (b, 0, 0))
    q_block = lambda w: pl.BlockSpec((tq, w), lambda b, q: (b * nq + q, 0))
    cache_block = lambda w: pl.BlockSpec((None, None, PAST_LEN, w), lambda b, q: (b, layer, 0, 0))
    return pl.pallas_call(
        functools.partial(_lat_mix_kernel, layer),
        grid=(DEC_BATCH, nq),
        in_specs=[
            pl.BlockSpec(memory_space=pl.ANY),
            seq_block(256), q_block(256), seq_block(128), seq_block_t(128),
            q_block(256), seq_block(256), seq_block_t(256),
            q_block(256), q_block(256),
            cache_block(128), cache_block(128), cache_block(256), cache_block(256),
            _layer_block((1, 256), layer),
            _layer_block((4, DIFF_QK_DIM), layer),
            pl.BlockSpec((tq, 2 * DEC_SEQ), lambda b, q: (q, 0)),
            _full((BRANCH_W, BRANCH_W)),
            _full((BRANCH_W, BRANCH_W)),
            _layer_block((BRANCH_W, BRANCH_W), layer),
            _layer_block((SGU_GROUPS * SGU_CHUNK, SGU_CHUNK), layer),
            _layer_block((SGU_CHUNK, BRANCH_W), layer),
        ],
        out_specs=pl.BlockSpec((tq, D_MODEL), lambda b, q: (ctx_tiles + b * nq + q, 0)),
        out_shape=jax.ShapeDtypeStruct((ALL_ROWS, D_MODEL), BF16),
        scratch_shapes=[
            pltpu.VMEM((n_chunks, KEY_CHUNK, 128), BF16), pltpu.VMEM((n_chunks, 128, KEY_CHUNK), BF16),
            pltpu.VMEM((n_chunks, KEY_CHUNK, 256), BF16), pltpu.VMEM((n_chunks, 256, KEY_CHUNK), BF16),
            pltpu.VMEM((2 * DEC_SEQ, BRANCH_W), BF16),
            pltpu.VMEM((3, 2, n_chunks, KEY_CHUNK, tq), F32),
            pltpu.VMEM((2 * BRANCH_W, tq), F32),
        ],
        input_output_aliases={0: 0},
        compiler_params=_params("arbitrary", "arbitrary"),
        name=f"lat_mix_{layer}",
    )(br, *pre, *caches, p["gdiff"], p["lamv"],
      p["dft_lat"], p["bdc"], p["bds"], p["w_fourier"], p["wsp"], p["bsp"])


def _merge_call(layer, x_ctx, x_lat, lat_row0, br, mods, p):
    tm = ROW_TM
    ctx_tiles = CTX_ROWS // tm
    lat_tile0 = lat_row0 // tm
    return pl.pallas_call(
        _merge_kernel,
        grid=(ALL_ROWS // tm,),
        in_specs=[
            pl.BlockSpec((tm, D_MODEL), lambda i: (jnp.minimum(i, ctx_tiles - 1), 0)),
            pl.BlockSpec((tm, D_MODEL), lambda i: (lat_tile0 + jnp.maximum(i - ctx_tiles, 0), 0)),
            pl.BlockSpec((tm, D_MODEL), lambda i: (i, 0)),
            pl.BlockSpec((None, None, N_MOD, D_MODEL), _row_mod_index(layer, tm)),
            _layer_block((1, D_MODEL), layer),
            _layer_block((D_MODEL, N_BRANCH * D_MODEL), layer),
            _layer_block((N_BRANCH, BRANCH_W, D_MODEL), layer),
            _layer_block((D_MODEL, D_MODEL), layer),
        ],
        out_specs=pl.BlockSpec((tm, D_MODEL), lambda i: (i, 0)),
        out_shape=jax.ShapeDtypeStruct((ALL_ROWS, D_MODEL), F32),
        compiler_params=_params("arbitrary"),
        name=f"merge_{layer}",
    )(x_ctx, x_lat, br, mods, p["norm1_g"], p["w_gate"], p["w_branch"], p["w_out"])


def _mlp_call(layer, x, mods, p, final_g=None):
    tm = ROW_TM
    ctx_tiles = CTX_ROWS // tm
    in_specs = [
        pl.BlockSpec((tm, D_MODEL), lambda i: (i, 0)),
        pl.BlockSpec((None, None, N_MOD, D_MODEL), _row_mod_index(layer, tm)),
        _layer_block((1, D_MODEL), layer),
        _layer_block((D_MODEL, D_FF), layer),
        _layer_block((D_FF, D_MODEL), layer),
    ]
    args = [x, mods, p["norm2_g"], p["w_mlp1"], p["w_mlp2"]]
    if final_g is None:
        body = _mlp_kernel
        out_specs = pl.BlockSpec((tm, D_MODEL), lambda i: (i, 0))
        out_shape = jax.ShapeDtypeStruct((ALL_ROWS, D_MODEL), F32)
    else:
        body = _mlp_final_kernel
        in_specs.append(_full((1, D_MODEL)))
        args.append(final_g)
        out_specs = [
            pl.BlockSpec((tm, D_MODEL), lambda i: (jnp.minimum(i, ctx_tiles - 1), 0)),
            pl.BlockSpec((tm, D_MODEL), lambda i: (jnp.maximum(i - ctx_tiles, 0), 0)),
        ]
        out_shape = [jax.ShapeDtypeStruct((CTX_ROWS, D_MODEL), F32),
                     jax.ShapeDtypeStruct((LAT_ROWS, D_MODEL), F32)]
    return pl.pallas_call(
        body,
        grid=(ALL_ROWS // tm,),
        in_specs=in_specs,
        out_specs=out_specs,
        out_shape=out_shape,
        compiler_params=_params("arbitrary"),
        name=f"mlp_{layer}",
    )(*args)


def _permute_q_heads(w, axis):
    shape = w.shape
    w = w.reshape(shape[:axis] + (GQA_HEADS, GQA_HEAD_DIM) + shape[axis + 1:])
    w = jnp.take(w, jnp.array(_Q_HEAD_ORDER), axis=axis)
    return w.reshape(shape)


def kernel(x_prompt, x_sample, c, cache_gqa_k, cache_gqa_v, cache_diff_k, cache_diff_v, c_ctx, w_ada, b_ada, norm1_g, norm2_g, w_in, w_fourier, q_norm_g, k_norm_g, lambda_q1, lambda_k1, lambda_q2, lambda_k2, diff_norm_g, sgu_norm_g, w_spatial, b_spatial, w_gate, w_branch, w_out, w_mlp1, w_mlp2, final_norm_g):
    w_in_p = jnp.concatenate(
        [w_in[..., :_QB0], _permute_q_heads(w_in[..., _QB0:_KB0], 2), w_in[..., _KB0:]], axis=-1)
    bdc, bds = _channel_dft_tables()
    cb, sb = _rope_tables(DEC_SEQ, GQA_HEAD_DIM)
    cc, sc = _rope_tables(DEC_SEQ, DIFF_QK_DIM)
    p = {
        "norm1_g": norm1_g.reshape(DEPTH, 1, D_MODEL),
        "norm2_g": norm2_g.reshape(DEPTH, 1, D_MODEL),
        "w_in": w_in_p.astype(BF16),
        "w_fourier": w_fourier.astype(BF16),
        "gq": jnp.tile(q_norm_g, (1, GQA_HEADS)).reshape(DEPTH, 1, 256),
        "gk": jnp.tile(k_norm_g, (1, GQA_KV_HEADS)).reshape(DEPTH, 1, 128),
        "gsgu": sgu_norm_g.reshape(DEPTH, 1, BRANCH_W),
        "gdiff": jnp.tile(diff_norm_g, (1, DIFF_HEADS)).reshape(DEPTH, 1, 256),
        "lamv": jnp.stack([lambda_q1, lambda_k1, lambda_q2, lambda_k2], axis=1),
        "wsp": w_spatial.reshape(DEPTH, SGU_GROUPS * SGU_CHUNK, SGU_CHUNK).astype(BF16),
        "bsp": jnp.repeat(jnp.swapaxes(b_spatial, 1, 2), BRANCH_W // SGU_GROUPS, axis=2),
        "w_gate": w_gate.astype(BF16),
        "w_branch": w_branch.astype(BF16),
        "w_out": w_out.astype(BF16),
        "w_mlp1": w_mlp1.astype(BF16),
        "w_mlp2": w_mlp2.astype(BF16),
        "dft_ctx": jnp.asarray(_dft_tables(SEQ)).astype(BF16),
        "dft_lat": jnp.asarray(_dft_tables(DEC_SEQ)).astype(BF16),
        "bdc": jnp.asarray(bdc).astype(BF16),
        "bds": jnp.asarray(bds).astype(BF16),
        "rope_cb": jnp.asarray(cb), "rope_sb": jnp.asarray(sb),
        "rope_cc": jnp.asarray(cc), "rope_sc": jnp.asarray(sc),
    }
    lat_caches = (
        cache_gqa_k.reshape(DEC_BATCH, DEPTH, PAST_LEN, 128),
        cache_gqa_v.reshape(DEC_BATCH, DEPTH, PAST_LEN, 128),
        cache_diff_k.reshape(DEC_BATCH, DEPTH, PAST_LEN, 256),
        cache_diff_v.reshape(DEC_BATCH, DEPTH, PAST_LEN, 256),
    )

    cond = jnp.concatenate(
        [c_ctx[None, :], c, jnp.zeros((COND_ROWS - 1 - DEC_BATCH, D_MODEL), F32)], axis=0)
    mods = _ada_call(cond, w_ada, b_ada)[:, :1 + DEC_BATCH].reshape(
        DEPTH, 1 + DEC_BATCH, N_MOD, D_MODEL)

    x_ctx, x_lat, lat_row0 = x_prompt.reshape(CTX_ROWS, D_MODEL), x_sample.reshape(LAT_ROWS, D_MODEL), 0
    new_caches = None
    for layer in range(DEPTH):
        br, new_caches = _ctx_call(layer, x_ctx, mods, p, new_caches)
        pre = _lat_pre_call(layer, x_lat, lat_row0, mods, p)
        br = _lat_mix_call(layer, br, pre, lat_caches, p)
        x = _merge_call(layer, x_ctx, x_lat, lat_row0, br, mods, p)
        if layer + 1 < DEPTH:
            x = _mlp_call(layer, x, mods, p)
            x_ctx, x_lat, lat_row0 = x, x, CTX_ROWS
    y_ctx, y_lat = _mlp_call(DEPTH - 1, x, mods, p, final_g=final_norm_g.reshape(1, D_MODEL))

    gk, gv, dk, dv = new_caches
    return (
        y_ctx.reshape(BATCH, SEQ, D_MODEL),
        y_lat.reshape(DEC_BATCH, DEC_SEQ, D_MODEL),
        gk.reshape(BATCH, DEPTH, SEQ, GQA_KV_HEADS, GQA_HEAD_DIM),
        gv.reshape(BATCH, DEPTH, SEQ, GQA_KV_HEADS, GQA_HEAD_DIM),
        dk.reshape(BATCH, DEPTH, SEQ, DIFF_HEADS, 2, DIFF_QK_DIM),
        dv.reshape(BATCH, DEPTH, SEQ, DIFF_HEADS, DIFF_V_DIM),
    )
```

```python
import functools
import math

import numpy as np
import jax
import jax.numpy as jnp
from jax import lax
from jax.experimental import pallas as pl
from jax.experimental.pallas import tpu as pltpu

F32 = jnp.float32
BF16 = jnp.bfloat16

D_MODEL = 1024
BATCH = 32
SEQ = 256
DEPTH = 4
DEC_BATCH = 2
DEC_SEQ = 2048
PAST_LEN = 512
GRID_W = 64
ROPE_THETA = 10000.0
EPS = 1e-6
N_BRANCH = 4
BRANCH_W = 256
FNET_GW = 64
GQA_HEADS = 4
GQA_KV_HEADS = 2
GQA_HEAD_DIM = 64
DIFF_HEADS = 4
DIFF_V_DIM = 64
DIFF_QK_DIM = 32
SGU_GROUPS = 4
SGU_CHUNK = 128
D_FF = 4 * D_MODEL
N_MOD = 6
IN_W = 2048

CTX_ROWS = BATCH * SEQ
LAT_ROWS = DEC_BATCH * DEC_SEQ
ALL_ROWS = CTX_ROWS + LAT_ROWS
KEYS_LAT = PAST_LEN + DEC_SEQ

LANES = 128
VMEM_LIMIT = 52 * 1024 * 1024

CTX_SEQS_PER_STEP = 4
LAT_PRE_TM = 512
LAT_TQ = 256
KEY_CHUNK = 256
ROW_TM = 512
COND_ROWS = 8
CAST_STEPS = 16

_A0, _QB0, _KB0, _VB0, _QC0, _KC0, _VC0, _DU0, _DV0 = 0, 256, 512, 640, 768, 1024, 1280, 1536, 1792
_Q_HEAD_ORDER = (0, 2, 1, 3)
_LOG2E = math.log2(math.e)
_GQA_Q_SCALE = _LOG2E / math.sqrt(GQA_HEAD_DIM)
_DIFF_Q_SCALE = _LOG2E / math.sqrt(DIFF_QK_DIM)


def _dft_tables(length):
    k = np.arange(length, dtype=np.int64)
    idx = (k[:, None] * k[None, :]) % length
    ang = 2.0 * np.pi * idx.astype(np.float64) / length
    return np.concatenate([np.cos(ang), -np.sin(ang)], axis=1).astype(np.float32)


def _channel_dft_tables():
    k = np.arange(FNET_GW, dtype=np.int64)
    idx = (k[:, None] * k[None, :]) % FNET_GW
    ang = 2.0 * np.pi * idx.astype(np.float64) / FNET_GW
    c = np.cos(ang) / math.sqrt(FNET_GW)
    s = np.sin(ang) / math.sqrt(FNET_GW)
    eye = np.eye(BRANCH_W // FNET_GW)
    return (np.kron(eye, c).astype(np.float32), np.kron(eye, s).astype(np.float32))


def _rope_tables(length, dim):
    rows = length // GRID_W
    row = np.repeat(np.arange(rows, dtype=np.float64), GRID_W)
    col = np.tile(np.arange(GRID_W, dtype=np.float64), rows)
    quarter = dim // 4
    inv = ROPE_THETA ** (-np.arange(quarter, dtype=np.float64) / quarter)
    ang = np.concatenate([row[:, None] * inv, col[:, None] * inv], axis=-1)
    c = np.concatenate([np.cos(ang), np.cos(ang)], axis=-1)
    s = np.concatenate([-np.sin(ang), np.sin(ang)], axis=-1)
    reps = LANES // dim
    return (np.tile(c, (1, reps)).astype(np.float32), np.tile(s, (1, reps)).astype(np.float32))


def _dot(a, b):
    return jnp.dot(a, b, preferred_element_type=F32)


def _dot_nt(a, b):
    return lax.dot_general(a, b, (((1,), (1,)), ((), ())), preferred_element_type=F32)


def _rms_mod(x, g, scale, shift):
    y = x * lax.rsqrt(jnp.mean(x * x, axis=-1, keepdims=True) + EPS) * g
    return y * (1.0 + scale) + shift


def _gelu(x):
    c = math.sqrt(2.0 / math.pi)
    return 0.5 * x * (1.0 + jnp.tanh(c * (x + 0.044715 * (x * x * x))))


def _group_ones(width, group):
    r = lax.broadcasted_iota(jnp.int32, (width, width), 0) // group
    c = lax.broadcasted_iota(jnp.int32, (width, width), 1) // group
    return (r == c).astype(BF16)


def _group_mean_sq(x, group):
    x2 = x * x
    hi = x2.astype(BF16)
    lo = (x2 - hi.astype(F32)).astype(BF16)
    ones = _group_ones(x.shape[-1], group)
    return (_dot(hi, ones) + _dot(lo, ones)) * (1.0 / group)


def _lane_mask(lo, hi):
    lane = lax.broadcasted_iota(jnp.int32, (1, LANES), 1)
    return (lane >= lo) & (lane < hi)


def _swap_halves(x, dim):
    n = x.shape[-1]
    half = dim // 2
    lane = lax.broadcasted_iota(jnp.int32, (1, n), 1)
    fwd = pltpu.roll(x, n - half, 1)
    bwd = pltpu.roll(x, half, 1)
    return jnp.where((lane % dim) < half, fwd, bwd)


def _rope(x, cos, sin, dim):
    reps = x.shape[-1] // LANES
    if reps > 1:
        cos = jnp.concatenate([cos] * reps, axis=1)
        sin = jnp.concatenate([sin] * reps, axis=1)
    return x * cos + _swap_halves(x, dim) * sin


def _pair_q_heads(q):
    b0, b1 = q[:, :LANES], q[:, LANES:]
    lo = _lane_mask(0, GQA_HEAD_DIM)
    r0 = pltpu.roll(b0, GQA_HEAD_DIM, 1)
    r1 = pltpu.roll(b1, GQA_HEAD_DIM, 1)
    return jnp.concatenate([jnp.where(lo, b0, r1), jnp.where(lo, r0, b1)], axis=1)


def _fold8(x, op):
    return op(x.reshape(x.shape[0] // 8, 8, x.shape[1]), axis=0)


class _Unit:
    def __init__(self, k_ref, k_lanes, q_blk, masks, vt_ref, v_rows, out_row0, is_diff):
        self.k_ref, self.k_lanes, self.vt_ref, self.v_rows = k_ref, k_lanes, vt_ref, v_rows
        self.out_row0, self.is_diff = out_row0, is_diff
        self.qm = [jnp.where(_lane_mask(lo, hi), q_blk, jnp.zeros_like(q_blk)) for lo, hi in masks]
        self.m = self.scale = None


def _make_units(qb, qc, kb_ref, kc_ref, vbt_ref, vct_ref):
    gqa_units, diff_units = [], []
    for pos, head in enumerate(_Q_HEAD_ORDER):
        blk, kv = pos // 2, pos % 2
        rows = slice(kv * GQA_HEAD_DIM, (kv + 1) * GQA_HEAD_DIM)
        gqa_units.append(_Unit(kb_ref, slice(0, LANES), qb[:, blk * LANES:(blk + 1) * LANES],
                               [(rows.start, rows.stop)], vbt_ref, rows, head * GQA_HEAD_DIM, False))
    for head in range(DIFF_HEADS):
        blk, base = head // 2, (head % 2) * DIFF_V_DIM
        lanes = slice(blk * LANES, (blk + 1) * LANES)
        diff_units.append(_Unit(kc_ref, lanes, qc[:, lanes],
                                [(base, base + DIFF_QK_DIM), (base + DIFF_QK_DIM, base + DIFF_V_DIM)],
                                vct_ref, slice(head * DIFF_V_DIM, (head + 1) * DIFF_V_DIM),
                                BRANCH_W + head * DIFF_V_DIM, True))
    return [u for pair in zip(gqa_units, diff_units) for u in pair]


def _attention_pipeline(units, lam, s_buf, ot_ref):
    n = len(units)
    tq = s_buf.shape[-1]
    for j in range(n + 2):
        ua = units[j] if j < n else None
        ub = units[j - 1] if 1 <= j <= n else None
        uc = units[j - 2] if 2 <= j <= n + 1 and units[j - 2].is_diff else None
        sa, sb, sc = j % 3, (j - 1) % 3, (j - 2) % 3
        init = {}
        if ua is not None:
            init["mx"] = [jnp.full((8, tq), -jnp.inf, F32) for _ in ua.qm]
        if ub is not None:
            init["l"] = [jnp.zeros((8, tq), F32) for _ in ub.qm]
            if not ub.is_diff:
                init["ob"] = jnp.zeros((DIFF_V_DIM, tq), F32)
        if uc is not None:
            init["oc"] = jnp.zeros((DIFF_V_DIM, tq), F32)

        def body(c, carry, ua=ua, ub=ub, uc=uc, sa=sa, sb=sb, sc=sc):
            new = dict(carry)
            if ua is not None:
                k_c = ua.k_ref[c, :, ua.k_lanes]
                mx = []
                for mi, qm in enumerate(ua.qm):
                    s = _dot_nt(k_c, qm)
                    s_buf[sa, mi, c] = s
                    mx.append(jnp.maximum(carry["mx"][mi], _fold8(s, jnp.max)))
                new["mx"] = mx
            if ub is not None:
                ls = []
                for mi in range(len(ub.qm)):
                    e = jnp.exp2(s_buf[sb, mi, c] - ub.m[mi])
                    ls.append(carry["l"][mi] + _fold8(e, jnp.sum))
                    if ub.is_diff:
                        s_buf[sb, mi, c] = e
                    else:
                        new["ob"] = carry["ob"] + _dot(ub.vt_ref[c, ub.v_rows, :], e.astype(BF16))
                new["l"] = ls
            if uc is not None:
                w = s_buf[sc, 0, c] - s_buf[sc, 1, c] * uc.scale[1]
                new["oc"] = carry["oc"] + _dot(uc.vt_ref[c, uc.v_rows, :], w.astype(BF16))
            return new

        out = lax.fori_loop(0, s_buf.shape[2], body, init, unroll=True)
        if ua is not None:
            ua.m = [jnp.max(mx, axis=0, keepdims=True) for mx in out["mx"]]
        if ub is not None:
            l = [jnp.sum(x, axis=0, keepdims=True) for x in out["l"]]
            if ub.is_diff:
                ub.scale = [1.0 / l[0], lam * l[0] / l[1]]
            else:
                ot_ref[ub.out_row0:ub.out_row0 + DIFF_V_DIM, :] = out["ob"] * (1.0 / l[0])
        if uc is not None:
            o = out["oc"] * uc.scale[0]
            ot_ref[uc.out_row0:uc.out_row0 + DIFF_V_DIM, :] = (
                o * lax.rsqrt(jnp.mean(o * o, axis=0, keepdims=True) + EPS))


def _lambda(lamv, layer):
    lam_init = 0.8 - 0.6 * math.exp(-0.3 * layer)
    a = jnp.sum(lamv[0:1] * lamv[1:2], axis=-1, keepdims=True)
    b = jnp.sum(lamv[2:3] * lamv[3:4], axis=-1, keepdims=True)
    return jnp.exp(a) - jnp.exp(b) + lam_init, 1.0 - lam_init


def _sgu(du, dvn, wsp, bias):
    lane = lax.broadcasted_iota(jnp.int32, (1, BRANCH_W), 1)
    outs = []
    for n in range(du.shape[0] // SGU_CHUNK):
        rows = slice(n * SGU_CHUNK, (n + 1) * SGU_CHUNK)
        r = _dot(wsp, dvn[rows].astype(BF16))
        s = bias
        for g in range(SGU_GROUPS):
            gm = (lane >= g * 64) & (lane < (g + 1) * 64)
            s = s + jnp.where(gm, r[g * SGU_CHUNK:(g + 1) * SGU_CHUNK], 0.0)
        outs.append(du[rows] * s)
    return outs[0] if len(outs) == 1 else jnp.concatenate(outs, axis=0)


def _fourier_stage1(a, bdc, bds):
    return jnp.concatenate([_dot(a, bdc), _dot(a, bds)], axis=0).astype(BF16)


def _fourier_stage2(dft_rows, f, length, w_fourier):
    y = _dot(dft_rows, f) * (1.0 / math.sqrt(length))
    return _dot(y.astype(BF16), w_fourier)


def _ada_kernel(cond_ref, w_ref, b_ref, o_ref):
    c = cond_ref[...]
    s = c * jax.nn.sigmoid(c)
    o_ref[...] = _dot(s.astype(BF16), w_ref[...].astype(BF16)) + b_ref[...]


def _ctx_kernel(layer, x_ref, mod_ref, g1_ref, win_ref, gq_ref, gk_ref, gsgu_ref, gdiff_ref,
                lamv_ref, dft_ref, bdc_ref, bds_ref, wf_ref, wsp_ref, bsp_ref,
                br_ref, ok_ref, ov_ref, odk_ref, odv_ref,
                kb_s, vbt_s, kc_s, vct_s, s_buf, ot_s):
    x = x_ref[...]
    mod = mod_ref[...]
    h = _rms_mod(x, g1_ref[...], mod[1:2], mod[0:1]).astype(BF16)
    proj = _dot(h, win_ref[...])
    qb = proj[:, _QB0:_KB0]
    kb = proj[:, _KB0:_VB0]
    qb = _pair_q_heads(qb * lax.rsqrt(_group_mean_sq(qb, GQA_HEAD_DIM) + EPS) * gq_ref[...])
    kb = kb * lax.rsqrt(_group_mean_sq(kb, GQA_HEAD_DIM) + EPS) * gk_ref[...]
    vb = proj[:, _VB0:_QC0]
    qc = proj[:, _QC0:_KC0]
    kc = proj[:, _KC0:_VC0]
    vc = proj[:, _VC0:_DU0]
    du = _gelu(proj[:, _DU0:_DV0])
    dv = _gelu(proj[:, _DV0:IN_W])
    dvn = dv * lax.rsqrt(jnp.mean(dv * dv, axis=-1, keepdims=True) + EPS) * gsgu_ref[...]
    lam, lam_scale = _lambda(lamv_ref[...], layer)
    qb = (qb * _GQA_Q_SCALE).astype(BF16)
    qc = (qc * _DIFF_Q_SCALE).astype(BF16)
    a = proj[:, _A0:_QB0].astype(BF16)
    for s in range(CTX_SEQS_PER_STEP):
        rows = slice(s * SEQ, (s + 1) * SEQ)
        ok_ref[s] = kb[rows]
        ov_ref[s] = vb[rows]
        odk_ref[s] = kc[rows]
        odv_ref[s] = vc[rows]
        kb_s[s, 0] = kb[rows].astype(BF16)
        kc_s[s, 0] = kc[rows].astype(BF16)
        vbt_s[s, 0] = vb[rows].T.astype(BF16)
        vct_s[s, 0] = vc[rows].T.astype(BF16)
        f = _fourier_stage1(a[rows], bdc_ref[...], bds_ref[...])
        br_ref[rows, 0:256] = _fourier_stage2(dft_ref[...], f, SEQ, wf_ref[...]).astype(BF16)
        br_ref[rows, 768:1024] = _sgu(du[rows], dvn[rows], wsp_ref[...], bsp_ref[...]).astype(BF16)
    for s in range(CTX_SEQS_PER_STEP):
        rows = slice(s * SEQ, (s + 1) * SEQ)
        units = _make_units(qb[rows], qc[rows], kb_s.at[s], kc_s.at[s], vbt_s.at[s], vct_s.at[s])
        _attention_pipeline(units, lam, s_buf.at[s], ot_s.at[s])
        br_ref[rows, 256:512] = ot_s[s, 0:BRANCH_W, :].T.astype(BF16)
        br_ref[rows, 512:768] = (ot_s[s, BRANCH_W:2 * BRANCH_W, :].T
                                 * (gdiff_ref[...] * lam_scale)).astype(BF16)


def _lat_pre_kernel(x_ref, mod_ref, g1_ref, win_ref, gq_ref, gk_ref, gsgu_ref,
                    cb_ref, sb_ref, cc_ref, sc_ref,
                    a_ref, qb_ref, kb_ref, vbt_ref, qc_ref, kc_ref, vct_ref, du_ref, dvn_ref):
    x = x_ref[...]
    mod = mod_ref[...]
    h = _rms_mod(x, g1_ref[...], mod[1:2], mod[0:1]).astype(BF16)
    proj = _dot(h, win_ref[...])
    cb, sb, cc, sc = cb_ref[...], sb_ref[...], cc_ref[...], sc_ref[...]
    qb = proj[:, _QB0:_KB0]
    kb = proj[:, _KB0:_VB0]
    qb = qb * lax.rsqrt(_group_mean_sq(qb, GQA_HEAD_DIM) + EPS) * gq_ref[...]
    kb = kb * lax.rsqrt(_group_mean_sq(kb, GQA_HEAD_DIM) + EPS) * gk_ref[...]
    qb = _pair_q_heads(_rope(qb, cb, sb, GQA_HEAD_DIM) * _GQA_Q_SCALE)
    kb = _rope(kb, cb, sb, GQA_HEAD_DIM)
    qc = _rope(proj[:, _QC0:_KC0], cc, sc, DIFF_QK_DIM) * _DIFF_Q_SCALE
    kc = _rope(proj[:, _KC0:_VC0], cc, sc, DIFF_QK_DIM)
    dv = _gelu(proj[:, _DV0:IN_W])
    a_ref[...] = proj[:, _A0:_QB0].astype(BF16)
    qb_ref[...] = qb.astype(BF16)
    kb_ref[...] = kb.astype(BF16)
    vbt_ref[...] = proj[:, _VB0:_QC0].T.astype(BF16)
    qc_ref[...] = qc.astype(BF16)
    kc_ref[...] = kc.astype(BF16)
    vct_ref[...] = proj[:, _VC0:_DU0].T.astype(BF16)
    du_ref[...] = _gelu(proj[:, _DU0:_DV0])
    dvn_ref[...] = (dv * lax.rsqrt(jnp.mean(dv * dv, axis=-1, keepdims=True) + EPS)
                    * gsgu_ref[...]).astype(BF16)


def _lat_mix_kernel(layer, a_ref, qb_ref, kb_ref, vbt_ref, qc_ref, kc_ref, vct_ref,
                    du_ref, dvn_ref, ck_ref, cv_ref, cdk_ref, cdv_ref, gdiff_ref, lamv_ref,
                    dft_ref, bdc_ref, bds_ref, wf_ref, wsp_ref, bsp_ref,
                    br_ref, kb_s, vbt_s, kc_s, vct_s, f_s, s_buf, ot_s):
    kc = KEY_CHUNK
    past_chunks = PAST_LEN // kc

    @pl.when(pl.program_id(1) == 0)
    def _fill():
        for c in range(KEYS_LAT // kc):
            if c < past_chunks:
                rows = slice(c * kc, (c + 1) * kc)
                kb_s[c] = ck_ref[rows, :].astype(BF16)
                kc_s[c] = cdk_ref[rows, :].astype(BF16)
                vbt_s[c] = cv_ref[rows, :].T.astype(BF16)
                vct_s[c] = cdv_ref[rows, :].T.astype(BF16)
            else:
                rows = slice((c - past_chunks) * kc, (c - past_chunks + 1) * kc)
                kb_s[c] = kb_ref[rows, :]
                kc_s[c] = kc_ref[rows, :]
                vbt_s[c] = vbt_ref[:, rows]
                vct_s[c] = vct_ref[:, rows]
        f_s[...] = _fourier_stage1(a_ref[...], bdc_ref[...], bds_ref[...])

    lam, lam_scale = _lambda(lamv_ref[...], layer)
    br_ref[:, 0:256] = _fourier_stage2(dft_ref[...], f_s[...], DEC_SEQ, wf_ref[...]).astype(BF16)
    br_ref[:, 768:1024] = _sgu(du_ref[...], dvn_ref[...].astype(F32), wsp_ref[...],
                               bsp_ref[...]).astype(BF16)

    units = _make_units(qb_ref[...], qc_ref[...], kb_s, kc_s, vbt_s, vct_s)
    _attention_pipeline(units, lam, s_buf, ot_s)
    br_ref[:, 256:512] = ot_s[0:BRANCH_W, :].T.astype(BF16)
    br_ref[:, 512:768] = (ot_s[BRANCH_W:2 * BRANCH_W, :].T * (gdiff_ref[...] * lam_scale)).astype(BF16)


def _cast_slabs(n_cast, refs):
    @pl.when(pl.program_id(0) < CAST_STEPS)
    def _():
        for src, dst in zip(refs[:n_cast], refs[n_cast:]):
            dst[...] = src[...].astype(BF16)


def _merge_kernel(n_cast, xc_ref, xl_ref, brc_ref, brl_ref, mod_ref, g1_ref, wg_ref, wb_ref, wo_ref,
                  *rest):
    cast_src, o_ref, cast_dst = rest[:n_cast], rest[n_cast], rest[n_cast + 1:]
    is_ctx = pl.program_id(0) < CTX_ROWS // ROW_TM
    x = jnp.where(is_ctx, xc_ref[...], xl_ref[...])
    br = jnp.where(is_ctx, brc_ref[...], brl_ref[...])
    mod = mod_ref[...]
    h = _rms_mod(x, g1_ref[...], mod[1:2], mod[0:1]).astype(BF16)
    merged = None
    for n in range(N_BRANCH):
        gate = jax.nn.sigmoid(_dot(h, wg_ref[:, n * D_MODEL:(n + 1) * D_MODEL]))
        term = gate * _dot(br[:, n * BRANCH_W:(n + 1) * BRANCH_W],
                           wb_ref[n * BRANCH_W:(n + 1) * BRANCH_W, :])
        merged = term if merged is None else merged + term
    o_ref[...] = x + mod[2:3] * _dot(merged.astype(BF16), wo_ref[...])
    _cast_slabs(n_cast, cast_src + cast_dst)


def _mlp_block(x, mod, g2, w1_ref, w2_ref):
    h = _rms_mod(x, g2, mod[4:5], mod[3:4]).astype(BF16)
    acc = None
    for n in range(D_FF // D_MODEL):
        cols = slice(n * D_MODEL, (n + 1) * D_MODEL)
        hid = jnp.maximum(_dot(h, w1_ref[:, cols]), 0.0)
        term = _dot((hid * hid).astype(BF16), w2_ref[cols, :])
        acc = term if acc is None else acc + term
    return x + mod[5:6] * acc


def _mlp_kernel(n_cast, x_ref, mod_ref, g2_ref, w1_ref, w2_ref, *rest):
    cast_src, o_ref, cast_dst = rest[:n_cast], rest[n_cast], rest[n_cast + 1:]
    o_ref[...] = _mlp_block(x_ref[...], mod_ref[...], g2_ref[...], w1_ref, w2_ref)
    _cast_slabs(n_cast, cast_src + cast_dst)


def _mlp_final_kernel(x_ref, mod_ref, g2_ref, w1_ref, w2_ref, gf_ref, yc_ref, yl_ref):
    x = _mlp_block(x_ref[...], mod_ref[...], g2_ref[...], w1_ref, w2_ref)
    y = x * lax.rsqrt(jnp.mean(x * x, axis=-1, keepdims=True) + EPS) * gf_ref[...]
    is_ctx = pl.program_id(0) < CTX_ROWS // ROW_TM

    @pl.when(is_ctx)
    def _():
        yc_ref[...] = y

    @pl.when(jnp.logical_not(is_ctx))
    def _():
        yl_ref[...] = y


def _params(*sem):
    return pltpu.CompilerParams(dimension_semantics=sem, vmem_limit_bytes=VMEM_LIMIT)


def _full(shape):
    nd = len(shape)
    return pl.BlockSpec(shape, lambda *_: (0,) * nd)


def _layer_block(shape, layer):
    nd = len(shape)
    return pl.BlockSpec((None,) + tuple(shape), lambda *_: (layer,) + (0,) * nd)


def _resident(shape):
    nd = len(shape)
    return pl.BlockSpec(shape, lambda *_: (0,) * nd, pipeline_mode=pl.Buffered(1))


def _cast_job(w, layer):
    _, r, c = w.shape
    slab = r // CAST_STEPS
    step = lambda i: jnp.minimum(i, CAST_STEPS - 1)
    return (pl.BlockSpec((None, slab, c), lambda i: (layer, step(i), 0)),
            pl.BlockSpec((slab, c), lambda i: (step(i), 0)),
            jax.ShapeDtypeStruct((r, c), BF16))


def _row_mod_index(layer, tm):
    ctx_tiles = CTX_ROWS // tm
    tiles_per_seq = DEC_SEQ // tm

    def index(i):
        sel = jnp.where(i < ctx_tiles, 0, 1 + (i - ctx_tiles) // tiles_per_seq)
        return (layer, sel, 0, 0)
    return index


def _ada_call(cond, w_ada, b_ada):
    return pl.pallas_call(
        _ada_kernel,
        grid=(DEPTH, N_MOD),
        in_specs=[
            pl.BlockSpec((COND_ROWS, D_MODEL), lambda l, j: (0, 0)),
            pl.BlockSpec((None, D_MODEL, D_MODEL), lambda l, j: (l, 0, j)),
            pl.BlockSpec((None, 1, D_MODEL), lambda l, j: (l, 0, j)),
        ],
        out_specs=pl.BlockSpec((None, COND_ROWS, D_MODEL), lambda l, j: (l, 0, j)),
        out_shape=jax.ShapeDtypeStruct((DEPTH, COND_ROWS, N_MOD * D_MODEL), F32),
        compiler_params=_params("arbitrary", "arbitrary"),
        name="ada",
    )(cond, w_ada, b_ada.reshape(DEPTH, 1, N_MOD * D_MODEL))


def _ctx_call(layer, x, mods, p, w, caches):
    ns = CTX_SEQS_PER_STEP
    tm = ns * SEQ
    cache_shapes = [(BATCH, DEPTH, SEQ, 128), (BATCH, DEPTH, SEQ, 128),
                    (BATCH, DEPTH, SEQ, 256), (BATCH, DEPTH, SEQ, 256)]
    in_specs = [
        pl.BlockSpec((tm, D_MODEL), lambda i: (i, 0)),
        pl.BlockSpec((None, None, N_MOD, D_MODEL), lambda i: (layer, 0, 0, 0)),
        _layer_block((1, D_MODEL), layer),
        _resident((D_MODEL, IN_W)),
        _layer_block((1, 256), layer),
        _layer_block((1, 128), layer),
        _layer_block((1, 256), layer),
        _layer_block((1, 256), layer),
        _layer_block((4, DIFF_QK_DIM), layer),
        _full((SEQ, 2 * SEQ)),
        _full((BRANCH_W, BRANCH_W)),
        _full((BRANCH_W, BRANCH_W)),
        _layer_block((BRANCH_W, BRANCH_W), layer),
        _layer_block((SGU_GROUPS * SGU_CHUNK, SGU_CHUNK), layer),
        _layer_block((SGU_CHUNK, BRANCH_W), layer),
    ]
    args = [x, mods, p["norm1_g"], w["w_in"], p["gq"], p["gk"], p["gsgu"], p["gdiff"], p["lamv"],
            p["dft_ctx"], p["bdc"], p["bds"], p["w_fourier"], p["wsp"], p["bsp"]]
    aliases = {}
    if caches is not None:
        for j, cache in enumerate(caches):
            in_specs.append(pl.BlockSpec(memory_space=pl.ANY))
            aliases[len(args)] = 1 + j
            args.append(cache)
    kernel = functools.partial(_ctx_kernel, layer)
    if caches is not None:
        base = kernel

        def kernel(*refs):
            n_in = 15
            return base(*refs[:n_in], *refs[n_in + 4:])
    out_specs = [pl.BlockSpec((tm, D_MODEL), lambda i: (i, 0))] + [
        pl.BlockSpec((ns, None, SEQ, s[-1]), lambda i: (i, layer, 0, 0)) for s in cache_shapes]
    out_shape = [jax.ShapeDtypeStruct((CTX_ROWS, D_MODEL), BF16)] + [
        jax.ShapeDtypeStruct(s, F32) for s in cache_shapes]
    outs = pl.pallas_call(
        kernel,
        grid=(BATCH // ns,),
        in_specs=in_specs,
        out_specs=out_specs,
        out_shape=out_shape,
        scratch_shapes=[
            pltpu.VMEM((ns, 1, SEQ, 128), BF16), pltpu.VMEM((ns, 1, 128, SEQ), BF16),
            pltpu.VMEM((ns, 1, SEQ, 256), BF16), pltpu.VMEM((ns, 1, 256, SEQ), BF16),
            pltpu.VMEM((ns, 3, 2, 1, SEQ, SEQ), F32),
            pltpu.VMEM((ns, 2 * BRANCH_W, SEQ), F32),
        ],
        input_output_aliases=aliases,
        compiler_params=_params("arbitrary"),
        name=f"ctx_mix_{layer}",
    )(*args)
    return outs[0], tuple(outs[1:])


def _lat_pre_call(layer, x, row0, mods, p, w):
    tm = LAT_PRE_TM
    tile0 = row0 // tm
    tiles_per_seq = DEC_SEQ // tm
    rope_spec = pl.BlockSpec((tm, LANES), lambda i: (i % tiles_per_seq, 0))
    rows = lambda w, dt: (pl.BlockSpec((tm, w), lambda i: (i, 0)),
                          jax.ShapeDtypeStruct((LAT_ROWS, w), dt))
    cols = lambda w: (pl.BlockSpec((None, w, tm), lambda i: (i // tiles_per_seq, 0, i % tiles_per_seq)),
                      jax.ShapeDtypeStruct((DEC_BATCH, w, DEC_SEQ), BF16))
    outs = [rows(256, BF16), rows(256, BF16), rows(128, BF16), cols(128),
            rows(256, BF16), rows(256, BF16), cols(256), rows(256, F32), rows(256, BF16)]
    return pl.pallas_call(
        _lat_pre_kernel,
        grid=(LAT_ROWS // tm,),
        in_specs=[
            pl.BlockSpec((tm, D_MODEL), lambda i: (tile0 + i, 0)),
            pl.BlockSpec((None, None, N_MOD, D_MODEL), lambda i: (layer, 1 + i // tiles_per_seq, 0, 0)),
            _layer_block((1, D_MODEL), layer),
            _resident((D_MODEL, IN_W)),
            _layer_block((1, 256), layer),
            _layer_block((1, 128), layer),
            _layer_block((1, 256), layer),
            rope_spec, rope_spec, rope_spec, rope_spec,
        ],
        out_specs=[spec for spec, _ in outs],
        out_shape=[shape for _, shape in outs],
        compiler_params=_params("arbitrary"),
        name=f"lat_pre_{layer}",
    )(x, mods, p["norm1_g"], w["w_in"], p["gq"], p["gk"], p["gsgu"],
      p["rope_cb"], p["rope_sb"], p["rope_cc"], p["rope_sc"])


def _lat_mix_call(layer, pre, caches, p):
    tq = LAT_TQ
    nq = DEC_SEQ // tq
    n_chunks = KEYS_LAT // KEY_CHUNK
    seq_block = lambda w: pl.BlockSpec((DEC_SEQ, w), lambda b, q: (b, 0))
    seq_block_t = lambda w: pl.BlockSpec((None, w, DEC_SEQ), lambda b, q: (b, 0, 0))
    q_block = lambda w: pl.BlockSpec((tq, w), lambda b, q: (b * nq + q, 0))
    cache_block = lambda w: pl.BlockSpec((None, None, PAST_LEN, w), lambda b, q: (b, layer, 0, 0))
    return pl.pallas_call(
        functools.partial(_lat_mix_kernel, layer),
        grid=(DEC_BATCH, nq),
        in_specs=[
            seq_block(256), q_block(256), seq_block(128), seq_block_t(128),
            q_block(256), seq_block(256), seq_block_t(256),
            q_block(256), q_block(256),
            cache_block(128), cache_block(128), cache_block(256), cache_block(256),
            _layer_block((1, 256), layer),
            _layer_block((4, DIFF_QK_DIM), layer),
            pl.BlockSpec((tq, 2 * DEC_SEQ), lambda b, q: (q, 0)),
            _full((BRANCH_W, BRANCH_W)),
            _full((BRANCH_W, BRANCH_W)),
            _layer_block((BRANCH_W, BRANCH_W), layer),
            _layer_block((SGU_GROUPS * SGU_CHUNK, SGU_CHUNK), layer),
            _layer_block((SGU_CHUNK, BRANCH_W), layer),
        ],
        out_specs=q_block(D_MODEL),
        out_shape=jax.ShapeDtypeStruct((LAT_ROWS, D_MODEL), BF16),
        scratch_shapes=[
            pltpu.VMEM((n_chunks, KEY_CHUNK, 128), BF16), pltpu.VMEM((n_chunks, 128, KEY_CHUNK), BF16),
            pltpu.VMEM((n_chunks, KEY_CHUNK, 256), BF16), pltpu.VMEM((n_chunks, 256, KEY_CHUNK), BF16),
            pltpu.VMEM((2 * DEC_SEQ, BRANCH_W), BF16),
            pltpu.VMEM((3, 2, n_chunks, KEY_CHUNK, tq), F32),
            pltpu.VMEM((2 * BRANCH_W, tq), F32),
        ],
        compiler_params=_params("arbitrary", "arbitrary"),
        name=f"lat_mix_{layer}",
    )(*pre, *caches, p["gdiff"], p["lamv"],
      p["dft_lat"], p["bdc"], p["bds"], p["w_fourier"], p["wsp"], p["bsp"])


def _merge_call(layer, x_ctx, x_lat, lat_row0, br_ctx, br_lat, mods, p, w, casts):
    tm = ROW_TM
    ctx_tiles = CTX_ROWS // tm
    lat_tile0 = lat_row0 // tm
    ctx_tile = lambda i: (jnp.minimum(i, ctx_tiles - 1), 0)
    jobs = [_cast_job(c, layer + 1) for c in casts]
    outs = pl.pallas_call(
        functools.partial(_merge_kernel, len(jobs)),
        grid=(ALL_ROWS // tm,),
        in_specs=[
            pl.BlockSpec((tm, D_MODEL), ctx_tile),
            pl.BlockSpec((tm, D_MODEL), lambda i: (lat_tile0 + jnp.maximum(i - ctx_tiles, 0), 0)),
            pl.BlockSpec((tm, D_MODEL), ctx_tile),
            pl.BlockSpec((tm, D_MODEL), lambda i: (jnp.maximum(i - ctx_tiles, 0), 0)),
            pl.BlockSpec((None, None, N_MOD, D_MODEL), _row_mod_index(layer, tm)),
            _layer_block((1, D_MODEL), layer),
            _resident((D_MODEL, N_BRANCH * D_MODEL)),
            _resident((N_BRANCH * BRANCH_W, D_MODEL)),
            _resident((D_MODEL, D_MODEL)),
        ] + [j[0] for j in jobs],
        out_specs=[pl.BlockSpec((tm, D_MODEL), lambda i: (i, 0))] + [j[1] for j in jobs],
        out_shape=[jax.ShapeDtypeStruct((ALL_ROWS, D_MODEL), F32)] + [j[2] for j in jobs],
        compiler_params=_params("arbitrary"),
        name=f"merge_{layer}",
    )(x_ctx, x_lat, br_ctx, br_lat, mods, p["norm1_g"], w["w_gate"], w["w_branch"], w["w_out"], *casts)
    return outs[0], outs[1:]


def _mlp_call(layer, x, mods, p, w, casts=(), final_g=None):
    tm = ROW_TM
    ctx_tiles = CTX_ROWS // tm
    in_specs = [
        pl.BlockSpec((tm, D_MODEL), lambda i: (i, 0)),
        pl.BlockSpec((None, None, N_MOD, D_MODEL), _row_mod_index(layer, tm)),
        _layer_block((1, D_MODEL), layer),
        _resident((D_MODEL, D_FF)),
        _resident((D_FF, D_MODEL)),
    ]
    args = [x, mods, p["norm2_g"], w["w_mlp1"], w["w_mlp2"]]
    if final_g is None:
        jobs = [_cast_job(c, layer + 1) for c in casts]
        body = functools.partial(_mlp_kernel, len(jobs))
        in_specs += [j[0] for j in jobs]
        args += list(casts)
        out_specs = [pl.BlockSpec((tm, D_MODEL), lambda i: (i, 0))] + [j[1] for j in jobs]
        out_shape = [jax.ShapeDtypeStruct((ALL_ROWS, D_MODEL), F32)] + [j[2] for j in jobs]
    else:
        body = _mlp_final_kernel
        in_specs.append(_full((1, D_MODEL)))
        args.append(final_g)
        out_specs = [
            pl.BlockSpec((tm, D_MODEL), lambda i: (jnp.minimum(i, ctx_tiles - 1), 0)),
            pl.BlockSpec((tm, D_MODEL), lambda i: (jnp.maximum(i - ctx_tiles, 0), 0)),
        ]
        out_shape = [jax.ShapeDtypeStruct((CTX_ROWS, D_MODEL), F32),
                     jax.ShapeDtypeStruct((LAT_ROWS, D_MODEL), F32)]
    return pl.pallas_call(
        body,
        grid=(ALL_ROWS // tm,),
        in_specs=in_specs,
        out_specs=out_specs,
        out_shape=out_shape,
        compiler_params=_params("arbitrary"),
        name=f"mlp_{layer}",
    )(*args)


def kernel(x_prompt, x_sample, c, cache_gqa_k, cache_gqa_v, cache_diff_k, cache_diff_v, c_ctx, w_ada, b_ada, norm1_g, norm2_g, w_in, w_fourier, q_norm_g, k_norm_g, lambda_q1, lambda_k1, lambda_q2, lambda_k2, diff_norm_g, sgu_norm_g, w_spatial, b_spatial, w_gate, w_branch, w_out, w_mlp1, w_mlp2, final_norm_g):
    bdc, bds = _channel_dft_tables()
    cb, sb = _rope_tables(DEC_SEQ, GQA_HEAD_DIM)
    cc, sc = _rope_tables(DEC_SEQ, DIFF_QK_DIM)
    p = {
        "norm1_g": norm1_g.reshape(DEPTH, 1, D_MODEL),
        "norm2_g": norm2_g.reshape(DEPTH, 1, D_MODEL),
        "w_fourier": w_fourier.astype(BF16),
        "gq": jnp.tile(q_norm_g, (1, GQA_HEADS)).reshape(DEPTH, 1, 256),
        "gk": jnp.tile(k_norm_g, (1, GQA_KV_HEADS)).reshape(DEPTH, 1, 128),
        "gsgu": sgu_norm_g.reshape(DEPTH, 1, BRANCH_W),
        "gdiff": jnp.tile(diff_norm_g, (1, DIFF_HEADS)).reshape(DEPTH, 1, 256),
        "lamv": jnp.stack([lambda_q1, lambda_k1, lambda_q2, lambda_k2], axis=1),
        "wsp": w_spatial.reshape(DEPTH, SGU_GROUPS * SGU_CHUNK, SGU_CHUNK).astype(BF16),
        "bsp": jnp.repeat(jnp.swapaxes(b_spatial, 1, 2), BRANCH_W // SGU_GROUPS, axis=2),
        "dft_ctx": jnp.asarray(_dft_tables(SEQ)).astype(BF16),
        "dft_lat": jnp.asarray(_dft_tables(DEC_SEQ)).astype(BF16),
        "bdc": jnp.asarray(bdc).astype(BF16),
        "bds": jnp.asarray(bds).astype(BF16),
        "rope_cb": jnp.asarray(cb), "rope_sb": jnp.asarray(sb),
        "rope_cc": jnp.asarray(cc), "rope_sc": jnp.asarray(sc),
    }
    lat_caches = (
        cache_gqa_k.reshape(DEC_BATCH, DEPTH, PAST_LEN, 128),
        cache_gqa_v.reshape(DEC_BATCH, DEPTH, PAST_LEN, 128),
        cache_diff_k.reshape(DEC_BATCH, DEPTH, PAST_LEN, 256),
        cache_diff_v.reshape(DEC_BATCH, DEPTH, PAST_LEN, 256),
    )

    cond = jnp.concatenate(
        [c_ctx[None, :], c, jnp.zeros((COND_ROWS - 1 - DEC_BATCH, D_MODEL), F32)], axis=0)
    mods = _ada_call(cond, w_ada, b_ada)[:, :1 + DEC_BATCH].reshape(
        DEPTH, 1 + DEC_BATCH, N_MOD, D_MODEL)

    merge_casts = {"w_in": w_in, "w_gate": w_gate}
    mlp_casts = {"w_branch": w_branch.reshape(DEPTH, N_BRANCH * BRANCH_W, D_MODEL), "w_out": w_out,
                 "w_mlp1": w_mlp1, "w_mlp2": w_mlp2}
    w = {k: v[0].astype(BF16) for k, v in {**merge_casts, **mlp_casts}.items()}

    x_ctx, x_lat, lat_row0 = x_prompt.reshape(CTX_ROWS, D_MODEL), x_sample.reshape(LAT_ROWS, D_MODEL), 0
    new_caches = None
    for layer in range(DEPTH):
        last = layer + 1 == DEPTH
        br_ctx, new_caches = _ctx_call(layer, x_ctx, mods, p, w, new_caches)
        pre = _lat_pre_call(layer, x_lat, lat_row0, mods, p, w)
        br_lat = _lat_mix_call(layer, pre, lat_caches, p)
        x, cast_a = _merge_call(layer, x_ctx, x_lat, lat_row0, br_ctx, br_lat, mods, p, w,
                                [] if last else list(merge_casts.values()))
        if last:
            y_ctx, y_lat = _mlp_call(layer, x, mods, p, w, final_g=final_norm_g.reshape(1, D_MODEL))
        else:
            outs = _mlp_call(layer, x, mods, p, w, list(mlp_casts.values()))
            x, cast_b = outs[0], outs[1:]
            w = dict(zip(list(merge_casts) + list(mlp_casts), list(cast_a) + list(cast_b)))
            x_ctx, x_lat, lat_row0 = x, x, CTX_ROWS

    gk, gv, dk, dv = new_caches
    return (
        y_ctx.reshape(BATCH, SEQ, D_MODEL),
        y_lat.reshape(DEC_BATCH, DEC_SEQ, D_MODEL),
        gk.reshape(BATCH, DEPTH, SEQ, GQA_KV_HEADS, GQA_HEAD_DIM),
        gv.reshape(BATCH, DEPTH, SEQ, GQA_KV_HEADS, GQA_HEAD_DIM),
        dk.reshape(BATCH, DEPTH, SEQ, DIFF_HEADS, 2, DIFF_QK_DIM),
        dv.reshape(BATCH, DEPTH, SEQ, DIFF_HEADS, DIFF_V_DIM),
    )
```

```python
import functools
import math

import numpy as np
import jax
import jax.numpy as jnp
from jax import lax
from jax.experimental import pallas as pl
from jax.experimental.pallas import tpu as pltpu

F32 = jnp.float32
BF16 = jnp.bfloat16

D_MODEL = 1024
BATCH = 32
SEQ = 256
DEPTH = 4
DEC_BATCH = 2
DEC_SEQ = 2048
PAST_LEN = 512
GRID_W = 64
ROPE_THETA = 10000.0
EPS = 1e-6
N_BRANCH = 4
BRANCH_W = 256
FNET_GW = 64
GQA_HEADS = 4
GQA_KV_HEADS = 2
GQA_HEAD_DIM = 64
DIFF_HEADS = 4
DIFF_V_DIM = 64
DIFF_QK_DIM = 32
SGU_GROUPS = 4
SGU_CHUNK = 128
D_FF = 4 * D_MODEL
N_MOD = 6
IN_W = 2048

CTX_ROWS = BATCH * SEQ
LAT_ROWS = DEC_BATCH * DEC_SEQ
ALL_ROWS = CTX_ROWS + LAT_ROWS
KEYS_LAT = PAST_LEN + DEC_SEQ

LANES = 128
VMEM_LIMIT = 52 * 1024 * 1024

CTX_SEQS_PER_STEP = 4
LAT_PRE_TM = 512
LAT_TQ = 256
KEY_CHUNK = 256
ROW_TM = 512
COND_ROWS = 8
CAST_STEPS = 16

_A0, _QB0, _KB0, _VB0, _QC0, _KC0, _VC0, _DU0, _DV0 = 0, 256, 512, 640, 768, 1024, 1280, 1536, 1792
_Q_HEAD_ORDER = (0, 2, 1, 3)
_LOG2E = math.log2(math.e)
_GQA_Q_SCALE = _LOG2E / math.sqrt(GQA_HEAD_DIM)
_DIFF_Q_SCALE = _LOG2E / math.sqrt(DIFF_QK_DIM)


def _dft_tables(length):
    k = np.arange(length, dtype=np.int64)
    idx = (k[:, None] * k[None, :]) % length
    ang = 2.0 * np.pi * idx.astype(np.float64) / length
    return np.concatenate([np.cos(ang), -np.sin(ang)], axis=1).astype(np.float32)


def _channel_dft_tables():
    k = np.arange(FNET_GW, dtype=np.int64)
    idx = (k[:, None] * k[None, :]) % FNET_GW
    ang = 2.0 * np.pi * idx.astype(np.float64) / FNET_GW
    c = np.cos(ang) / math.sqrt(FNET_GW)
    s = np.sin(ang) / math.sqrt(FNET_GW)
    eye = np.eye(BRANCH_W // FNET_GW)
    return (np.kron(eye, c).astype(np.float32), np.kron(eye, s).astype(np.float32))


def _rope_tables(length, dim):
    rows = length // GRID_W
    row = np.repeat(np.arange(rows, dtype=np.float64), GRID_W)
    col = np.tile(np.arange(GRID_W, dtype=np.float64), rows)
    quarter = dim // 4
    inv = ROPE_THETA ** (-np.arange(quarter, dtype=np.float64) / quarter)
    ang = np.concatenate([row[:, None] * inv, col[:, None] * inv], axis=-1)
    c = np.concatenate([np.cos(ang), np.cos(ang)], axis=-1)
    s = np.concatenate([-np.sin(ang), np.sin(ang)], axis=-1)
    reps = LANES // dim
    return (np.tile(c, (1, reps)).astype(np.float32), np.tile(s, (1, reps)).astype(np.float32))


def _dot(a, b):
    return jnp.dot(a, b, preferred_element_type=F32)


def _dot_nt(a, b):
    return lax.dot_general(a, b, (((1,), (1,)), ((), ())), preferred_element_type=F32)


def _rms_mod(x, g, scale, shift):
    y = x * lax.rsqrt(jnp.mean(x * x, axis=-1, keepdims=True) + EPS) * g
    return y * (1.0 + scale) + shift


def _gelu(x):
    c = math.sqrt(2.0 / math.pi)
    return 0.5 * x * (1.0 + jnp.tanh(c * (x + 0.044715 * (x * x * x))))


def _group_ones(width, group):
    r = lax.broadcasted_iota(jnp.int32, (width, width), 0) // group
    c = lax.broadcasted_iota(jnp.int32, (width, width), 1) // group
    return (r == c).astype(BF16)


def _group_mean_sq(x, group):
    x2 = x * x
    hi = x2.astype(BF16)
    lo = (x2 - hi.astype(F32)).astype(BF16)
    ones = _group_ones(x.shape[-1], group)
    return (_dot(hi, ones) + _dot(lo, ones)) * (1.0 / group)


def _lane_mask(lo, hi):
    lane = lax.broadcasted_iota(jnp.int32, (1, LANES), 1)
    return (lane >= lo) & (lane < hi)


def _swap_halves(x, dim):
    n = x.shape[-1]
    half = dim // 2
    lane = lax.broadcasted_iota(jnp.int32, (1, n), 1)
    fwd = pltpu.roll(x, n - half, 1)
    bwd = pltpu.roll(x, half, 1)
    return jnp.where((lane % dim) < half, fwd, bwd)


def _rope(x, cos, sin, dim):
    reps = x.shape[-1] // LANES
    if reps > 1:
        cos = jnp.concatenate([cos] * reps, axis=1)
        sin = jnp.concatenate([sin] * reps, axis=1)
    return x * cos + _swap_halves(x, dim) * sin


def _pair_q_heads(q):
    b0, b1 = q[:, :LANES], q[:, LANES:]
    lo = _lane_mask(0, GQA_HEAD_DIM)
    r0 = pltpu.roll(b0, GQA_HEAD_DIM, 1)
    r1 = pltpu.roll(b1, GQA_HEAD_DIM, 1)
    return jnp.concatenate([jnp.where(lo, b0, r1), jnp.where(lo, r0, b1)], axis=1)


def _fold8(x, op):
    return op(x.reshape(x.shape[0] // 8, 8, x.shape[1]), axis=0)


class _Unit:
    def __init__(self, k_ref, k_lanes, q_blk, masks, vt_ref, v_rows, out_row0, is_diff):
        self.k_ref, self.k_lanes, self.vt_ref, self.v_rows = k_ref, k_lanes, vt_ref, v_rows
        self.out_row0, self.is_diff = out_row0, is_diff
        self.qm = [jnp.where(_lane_mask(lo, hi), q_blk, jnp.zeros_like(q_blk)) for lo, hi in masks]
        self.m = self.scale = None


def _make_units(qb, qc, kb_ref, kc_ref, vbt_ref, vct_ref):
    gqa_units, diff_units = [], []
    for pos, head in enumerate(_Q_HEAD_ORDER):
        blk, kv = pos // 2, pos % 2
        rows = slice(kv * GQA_HEAD_DIM, (kv + 1) * GQA_HEAD_DIM)
        gqa_units.append(_Unit(kb_ref, slice(0, LANES), qb[:, blk * LANES:(blk + 1) * LANES],
                               [(rows.start, rows.stop)], vbt_ref, rows, head * GQA_HEAD_DIM, False))
    for head in range(DIFF_HEADS):
        blk, base = head // 2, (head % 2) * DIFF_V_DIM
        lanes = slice(blk * LANES, (blk + 1) * LANES)
        diff_units.append(_Unit(kc_ref, lanes, qc[:, lanes],
                                [(base, base + DIFF_QK_DIM), (base + DIFF_QK_DIM, base + DIFF_V_DIM)],
                                vct_ref, slice(head * DIFF_V_DIM, (head + 1) * DIFF_V_DIM),
                                BRANCH_W + head * DIFF_V_DIM, True))
    return [u for pair in zip(gqa_units, diff_units) for u in pair]


def _attention_pipeline(units, lam, s_buf, ot_ref):
    n = len(units)
    tq = s_buf.shape[-1]
    for j in range(n + 2):
        ua = units[j] if j < n else None
        ub = units[j - 1] if 1 <= j <= n else None
        uc = units[j - 2] if 2 <= j <= n + 1 and units[j - 2].is_diff else None
        sa, sb, sc = j % 3, (j - 1) % 3, (j - 2) % 3
        init = {}
        if ua is not None:
            init["mx"] = [jnp.full((8, tq), -jnp.inf, F32) for _ in ua.qm]
        if ub is not None:
            init["l"] = [jnp.zeros((8, tq), F32) for _ in ub.qm]
            if not ub.is_diff:
                init["ob"] = jnp.zeros((DIFF_V_DIM, tq), F32)
        if uc is not None:
            init["oc"] = jnp.zeros((DIFF_V_DIM, tq), F32)

        def body(c, carry, ua=ua, ub=ub, uc=uc, sa=sa, sb=sb, sc=sc):
            new = dict(carry)
            if ua is not None:
                k_c = ua.k_ref[c, :, ua.k_lanes]
                mx = []
                for mi, qm in enumerate(ua.qm):
                    s = _dot_nt(k_c, qm)
                    s_buf[sa, mi, c] = s
                    mx.append(jnp.maximum(carry["mx"][mi], _fold8(s, jnp.max)))
                new["mx"] = mx
            if ub is not None:
                ls = []
                for mi in range(len(ub.qm)):
                    e = jnp.exp2(s_buf[sb, mi, c] - ub.m[mi])
                    ls.append(carry["l"][mi] + _fold8(e, jnp.sum))
                    if ub.is_diff:
                        s_buf[sb, mi, c] = e
                    else:
                        new["ob"] = carry["ob"] + _dot(ub.vt_ref[c, ub.v_rows, :], e.astype(BF16))
                new["l"] = ls
            if uc is not None:
                w = s_buf[sc, 0, c] - s_buf[sc, 1, c] * uc.scale[1]
                new["oc"] = carry["oc"] + _dot(uc.vt_ref[c, uc.v_rows, :], w.astype(BF16))
            return new

        out = lax.fori_loop(0, s_buf.shape[2], body, init, unroll=True)
        if ua is not None:
            ua.m = [jnp.max(mx, axis=0, keepdims=True) for mx in out["mx"]]
        if ub is not None:
            l = [jnp.sum(x, axis=0, keepdims=True) for x in out["l"]]
            if ub.is_diff:
                ub.scale = [1.0 / l[0], lam * l[0] / l[1]]
            else:
                ot_ref[ub.out_row0:ub.out_row0 + DIFF_V_DIM, :] = out["ob"] * (1.0 / l[0])
        if uc is not None:
            o = out["oc"] * uc.scale[0]
            ot_ref[uc.out_row0:uc.out_row0 + DIFF_V_DIM, :] = (
                o * lax.rsqrt(jnp.mean(o * o, axis=0, keepdims=True) + EPS))


def _lambda(lamv, layer):
    lam_init = 0.8 - 0.6 * math.exp(-0.3 * layer)
    a = jnp.sum(lamv[0:1] * lamv[1:2], axis=-1, keepdims=True)
    b = jnp.sum(lamv[2:3] * lamv[3:4], axis=-1, keepdims=True)
    return jnp.exp(a) - jnp.exp(b) + lam_init, 1.0 - lam_init


def _sgu(du, dvn, wsp, bias):
    lane = lax.broadcasted_iota(jnp.int32, (1, BRANCH_W), 1)
    outs = []
    for n in range(du.shape[0] // SGU_CHUNK):
        rows = slice(n * SGU_CHUNK, (n + 1) * SGU_CHUNK)
        r = _dot(wsp, dvn[rows].astype(BF16))
        s = bias
        for g in range(SGU_GROUPS):
            gm = (lane >= g * 64) & (lane < (g + 1) * 64)
            s = s + jnp.where(gm, r[g * SGU_CHUNK:(g + 1) * SGU_CHUNK], 0.0)
        outs.append(du[rows] * s)
    return outs[0] if len(outs) == 1 else jnp.concatenate(outs, axis=0)


def _fourier_stage1(a, bdc, bds):
    return jnp.concatenate([_dot(a, bdc), _dot(a, bds)], axis=0).astype(BF16)


def _fourier_stage2(dft_rows, f, length, w_fourier):
    y = _dot(dft_rows, f) * (1.0 / math.sqrt(length))
    return _dot(y.astype(BF16), w_fourier)


def _ada_kernel(cond_ref, w_ref, b_ref, o_ref):
    c = cond_ref[...]
    s = c * jax.nn.sigmoid(c)
    o_ref[...] = _dot(s.astype(BF16), w_ref[...].astype(BF16)) + b_ref[...]


def _ctx_kernel(layer, x_ref, mod_ref, g1_ref, win_ref, gq_ref, gk_ref, gsgu_ref, gdiff_ref,
                lamv_ref, dft_ref, bdc_ref, bds_ref, wf_ref, wsp_ref, bsp_ref,
                br_ref, ok_ref, ov_ref, odk_ref, odv_ref,
                kb_s, vbt_s, kc_s, vct_s, s_buf, ot_s):
    x = x_ref[...]
    mod = mod_ref[...]
    h = _rms_mod(x, g1_ref[...], mod[1:2], mod[0:1]).astype(BF16)
    proj = _dot(h, win_ref[...])
    qb = proj[:, _QB0:_KB0]
    kb = proj[:, _KB0:_VB0]
    qb = _pair_q_heads(qb * lax.rsqrt(_group_mean_sq(qb, GQA_HEAD_DIM) + EPS) * gq_ref[...])
    kb = kb * lax.rsqrt(_group_mean_sq(kb, GQA_HEAD_DIM) + EPS) * gk_ref[...]
    vb = proj[:, _VB0:_QC0]
    qc = proj[:, _QC0:_KC0]
    kc = proj[:, _KC0:_VC0]
    vc = proj[:, _VC0:_DU0]
    du = _gelu(proj[:, _DU0:_DV0])
    dv = _gelu(proj[:, _DV0:IN_W])
    dvn = dv * lax.rsqrt(jnp.mean(dv * dv, axis=-1, keepdims=True) + EPS) * gsgu_ref[...]
    lam, lam_scale = _lambda(lamv_ref[...], layer)
    qb = (qb * _GQA_Q_SCALE).astype(BF16)
    qc = (qc * _DIFF_Q_SCALE).astype(BF16)
    a = proj[:, _A0:_QB0].astype(BF16)
    for s in range(CTX_SEQS_PER_STEP):
        rows = slice(s * SEQ, (s + 1) * SEQ)
        ok_ref[s] = kb[rows]
        ov_ref[s] = vb[rows]
        odk_ref[s] = kc[rows]
        odv_ref[s] = vc[rows]
        kb_s[s, 0] = kb[rows].astype(BF16)
        kc_s[s, 0] = kc[rows].astype(BF16)
        vbt_s[s, 0] = vb[rows].T.astype(BF16)
        vct_s[s, 0] = vc[rows].T.astype(BF16)
        f = _fourier_stage1(a[rows], bdc_ref[...], bds_ref[...])
        br_ref[rows, 0:256] = _fourier_stage2(dft_ref[...], f, SEQ, wf_ref[...]).astype(BF16)
        br_ref[rows, 768:1024] = _sgu(du[rows], dvn[rows], wsp_ref[...], bsp_ref[...]).astype(BF16)
    for s in range(CTX_SEQS_PER_STEP):
        rows = slice(s * SEQ, (s + 1) * SEQ)
        units = _make_units(qb[rows], qc[rows], kb_s.at[s], kc_s.at[s], vbt_s.at[s], vct_s.at[s])
        _attention_pipeline(units, lam, s_buf.at[s], ot_s.at[s])
        br_ref[rows, 256:512] = ot_s[s, 0:BRANCH_W, :].T.astype(BF16)
        br_ref[rows, 512:768] = (ot_s[s, BRANCH_W:2 * BRANCH_W, :].T
                                 * (gdiff_ref[...] * lam_scale)).astype(BF16)


def _lat_pre_kernel(x_ref, mod_ref, g1_ref, win_ref, gq_ref, gk_ref, gsgu_ref,
                    cb_ref, sb_ref, cc_ref, sc_ref,
                    a_ref, qb_ref, kb_ref, vbt_ref, qc_ref, kc_ref, vct_ref, du_ref, dvn_ref):
    x = x_ref[...]
    mod = mod_ref[...]
    h = _rms_mod(x, g1_ref[...], mod[1:2], mod[0:1]).astype(BF16)
    cb, sb, cc, sc = cb_ref[...], sb_ref[...], cc_ref[...], sc_ref[...]
    proj = lambda c0, c1: _dot(h, win_ref[:, c0:c1])
    qb = proj(_QB0, _KB0)
    kvb = proj(_KB0, _QC0)
    kb, vb = kvb[:, :LANES], kvb[:, LANES:]
    qb = qb * lax.rsqrt(_group_mean_sq(qb, GQA_HEAD_DIM) + EPS) * gq_ref[...]
    kb = kb * lax.rsqrt(_group_mean_sq(kb, GQA_HEAD_DIM) + EPS) * gk_ref[...]
    qb_ref[...] = _pair_q_heads(_rope(qb, cb, sb, GQA_HEAD_DIM) * _GQA_Q_SCALE).astype(BF16)
    kb_ref[...] = _rope(kb, cb, sb, GQA_HEAD_DIM).astype(BF16)
    vbt_ref[...] = vb.T.astype(BF16)
    qc = proj(_QC0, _KC0)
    kc = proj(_KC0, _VC0)
    qc_ref[...] = (_rope(qc, cc, sc, DIFF_QK_DIM) * _DIFF_Q_SCALE).astype(BF16)
    kc_ref[...] = _rope(kc, cc, sc, DIFF_QK_DIM).astype(BF16)
    vc = proj(_VC0, _DU0)
    du = proj(_DU0, _DV0)
    vct_ref[...] = vc.T.astype(BF16)
    du_ref[...] = _gelu(du)
    dv = _gelu(proj(_DV0, IN_W))
    a_ref[...] = proj(_A0, _QB0).astype(BF16)
    dvn_ref[...] = (dv * lax.rsqrt(jnp.mean(dv * dv, axis=-1, keepdims=True) + EPS)
                    * gsgu_ref[...]).astype(BF16)


def _lat_mix_kernel(layer, a_ref, qb_ref, kb_ref, vbt_ref, qc_ref, kc_ref, vct_ref,
                    du_ref, dvn_ref, ck_ref, cv_ref, cdk_ref, cdv_ref, gdiff_ref, lamv_ref,
                    dft_ref, bdc_ref, bds_ref, wf_ref, wsp_ref, bsp_ref,
                    br_ref, kb_s, vbt_s, kc_s, vct_s, f_s, s_buf, ot_s):
    kc = KEY_CHUNK
    past_chunks = PAST_LEN // kc

    @pl.when(pl.program_id(1) == 0)
    def _fill():
        for c in range(KEYS_LAT // kc):
            if c < past_chunks:
                rows = slice(c * kc, (c + 1) * kc)
                kb_s[c] = ck_ref[rows, :].astype(BF16)
                kc_s[c] = cdk_ref[rows, :].astype(BF16)
                vbt_s[c] = cv_ref[rows, :].T.astype(BF16)
                vct_s[c] = cdv_ref[rows, :].T.astype(BF16)
            else:
                rows = slice((c - past_chunks) * kc, (c - past_chunks + 1) * kc)
                kb_s[c] = kb_ref[rows, :]
                kc_s[c] = kc_ref[rows, :]
                vbt_s[c] = vbt_ref[:, rows]
                vct_s[c] = vct_ref[:, rows]
        f_s[...] = _fourier_stage1(a_ref[...], bdc_ref[...], bds_ref[...])

    lam, lam_scale = _lambda(lamv_ref[...], layer)
    units = _make_units(qb_ref[...], qc_ref[...], kb_s, kc_s, vbt_s, vct_s)
    _attention_pipeline(units, lam, s_buf, ot_s)
    br_ref[:, 0:256] = _fourier_stage2(dft_ref[...], f_s[...], DEC_SEQ, wf_ref[...]).astype(BF16)
    br_ref[:, 768:1024] = _sgu(du_ref[...], dvn_ref[...].astype(F32), wsp_ref[...],
                               bsp_ref[...]).astype(BF16)
    br_ref[:, 256:512] = ot_s[0:BRANCH_W, :].T.astype(BF16)
    br_ref[:, 512:768] = (ot_s[BRANCH_W:2 * BRANCH_W, :].T * (gdiff_ref[...] * lam_scale)).astype(BF16)


def _cast_slabs(n_cast, refs):
    @pl.when(pl.program_id(0) < CAST_STEPS)
    def _():
        for src, dst in zip(refs[:n_cast], refs[n_cast:]):
            dst[...] = src[...].astype(BF16)


def _merge_kernel(n_cast, xc_ref, xl_ref, brc_ref, brl_ref, mod_ref, g1_ref, wg_ref, wb_ref, wo_ref,
                  *rest):
    cast_src, o_ref, cast_dst = rest[:n_cast], rest[n_cast], rest[n_cast + 1:]
    is_ctx = pl.program_id(0) < CTX_ROWS // ROW_TM
    x = jnp.where(is_ctx, xc_ref[...], xl_ref[...])
    br = jnp.where(is_ctx, brc_ref[...], brl_ref[...])
    mod = mod_ref[...]
    bproj = [_dot(br[:, n * BRANCH_W:(n + 1) * BRANCH_W], wb_ref[n * BRANCH_W:(n + 1) * BRANCH_W, :])
             for n in range(N_BRANCH)]
    h = _rms_mod(x, g1_ref[...], mod[1:2], mod[0:1]).astype(BF16)
    merged = None
    for n in range(N_BRANCH):
        gate = jax.nn.sigmoid(_dot(h, wg_ref[:, n * D_MODEL:(n + 1) * D_MODEL]))
        term = gate * bproj[n]
        merged = term if merged is None else merged + term
    o_ref[...] = x + mod[2:3] * _dot(merged.astype(BF16), wo_ref[...])
    _cast_slabs(n_cast, cast_src + cast_dst)


def _mlp_block(x, mod, g2, w1_ref, w2_ref):
    h = _rms_mod(x, g2, mod[4:5], mod[3:4]).astype(BF16)
    acc = None
    for n in range(D_FF // D_MODEL):
        cols = slice(n * D_MODEL, (n + 1) * D_MODEL)
        hid = jnp.maximum(_dot(h, w1_ref[:, cols]), 0.0)
        term = _dot((hid * hid).astype(BF16), w2_ref[cols, :])
        acc = term if acc is None else acc + term
    return x + mod[5:6] * acc


def _mlp_kernel(n_cast, x_ref, mod_ref, g2_ref, w1_ref, w2_ref, cond_ref, wada_ref, bada_ref, *rest):
    cast_src, o_ref, modn_ref, cast_dst = rest[:n_cast], rest[n_cast], rest[n_cast + 1], rest[n_cast + 2:]
    o_ref[...] = _mlp_block(x_ref[...], mod_ref[...], g2_ref[...], w1_ref, w2_ref)
    _ada_kernel(cond_ref, wada_ref, bada_ref, modn_ref)
    _cast_slabs(n_cast, cast_src + cast_dst)


def _mlp_final_kernel(x_ref, mod_ref, g2_ref, w1_ref, w2_ref, gf_ref, yc_ref, yl_ref):
    x = _mlp_block(x_ref[...], mod_ref[...], g2_ref[...], w1_ref, w2_ref)
    y = x * lax.rsqrt(jnp.mean(x * x, axis=-1, keepdims=True) + EPS) * gf_ref[...]
    is_ctx = pl.program_id(0) < CTX_ROWS // ROW_TM

    @pl.when(is_ctx)
    def _():
        yc_ref[...] = y

    @pl.when(jnp.logical_not(is_ctx))
    def _():
        yl_ref[...] = y


def _params(*sem):
    return pltpu.CompilerParams(dimension_semantics=sem, vmem_limit_bytes=VMEM_LIMIT)


def _full(shape):
    nd = len(shape)
    return pl.BlockSpec(shape, lambda *_: (0,) * nd)


def _layer_block(shape, layer):
    nd = len(shape)
    return pl.BlockSpec((None,) + tuple(shape), lambda *_: (layer,) + (0,) * nd)


def _resident(shape):
    nd = len(shape)
    return pl.BlockSpec(shape, lambda *_: (0,) * nd, pipeline_mode=pl.Buffered(1))


def _cast_job(w, layer):
    _, r, c = w.shape
    slab = r // CAST_STEPS
    step = lambda i: jnp.minimum(i, CAST_STEPS - 1)
    return (pl.BlockSpec((None, slab, c), lambda i: (layer, step(i), 0)),
            pl.BlockSpec((slab, c), lambda i: (step(i), 0)),
            jax.ShapeDtypeStruct((r, c), BF16))


def _mod_spec(select):
    return pl.BlockSpec((None, N_MOD, D_MODEL), lambda *idx: (select(*idx), 0, 0))


def _row_mod_select(tm):
    ctx_tiles = CTX_ROWS // tm
    tiles_per_seq = DEC_SEQ // tm
    return lambda i: jnp.where(i < ctx_tiles, 0, 1 + (i - ctx_tiles) // tiles_per_seq)


def _ada_call(cond, w_ada, b_ada):
    return pl.pallas_call(
        _ada_kernel,
        grid=(N_MOD,),
        in_specs=[
            pl.BlockSpec((COND_ROWS, D_MODEL), lambda j: (0, 0)),
            pl.BlockSpec((None, D_MODEL, D_MODEL), lambda j: (0, 0, j)),
            pl.BlockSpec((None, 1, D_MODEL), lambda j: (0, 0, j)),
        ],
        out_specs=pl.BlockSpec((COND_ROWS, D_MODEL), lambda j: (0, j)),
        out_shape=jax.ShapeDtypeStruct((COND_ROWS, N_MOD * D_MODEL), F32),
        compiler_params=_params("arbitrary"),
        name="ada",
    )(cond, w_ada, b_ada)


def _ctx_call(layer, x, mods, p, w, caches):
    ns = CTX_SEQS_PER_STEP
    tm = ns * SEQ
    cache_shapes = [(BATCH, DEPTH, SEQ, 128), (BATCH, DEPTH, SEQ, 128),
                    (BATCH, DEPTH, SEQ, 256), (BATCH, DEPTH, SEQ, 256)]
    in_specs = [
        pl.BlockSpec((tm, D_MODEL), lambda i: (i, 0)),
        _mod_spec(lambda i: 0),
        _layer_block((1, D_MODEL), layer),
        _resident((D_MODEL, IN_W)),
        _layer_block((1, 256), layer),
        _layer_block((1, 128), layer),
        _layer_block((1, 256), layer),
        _layer_block((1, 256), layer),
        _layer_block((4, DIFF_QK_DIM), layer),
        _full((SEQ, 2 * SEQ)),
        _full((BRANCH_W, BRANCH_W)),
        _full((BRANCH_W, BRANCH_W)),
        _layer_block((BRANCH_W, BRANCH_W), layer),
        _layer_block((SGU_GROUPS * SGU_CHUNK, SGU_CHUNK), layer),
        _layer_block((SGU_CHUNK, BRANCH_W), layer),
    ]
    args = [x, mods, p["norm1_g"], w["w_in"], p["gq"], p["gk"], p["gsgu"], p["gdiff"], p["lamv"],
            p["dft_ctx"], p["bdc"], p["bds"], p["w_fourier"], p["wsp"], p["bsp"]]
    aliases = {}
    if caches is not None:
        for j, cache in enumerate(caches):
            in_specs.append(pl.BlockSpec(memory_space=pl.ANY))
            aliases[len(args)] = 1 + j
            args.append(cache)
    kernel = functools.partial(_ctx_kernel, layer)
    if caches is not None:
        base = kernel

        def kernel(*refs):
            n_in = 15
            return base(*refs[:n_in], *refs[n_in + 4:])
    out_specs = [pl.BlockSpec((tm, D_MODEL), lambda i: (i, 0))] + [
        pl.BlockSpec((ns, None, SEQ, s[-1]), lambda i: (i, layer, 0, 0)) for s in cache_shapes]
    out_shape = [jax.ShapeDtypeStruct((CTX_ROWS, D_MODEL), BF16)] + [
        jax.ShapeDtypeStruct(s, F32) for s in cache_shapes]
    outs = pl.pallas_call(
        kernel,
        grid=(BATCH // ns,),
        in_specs=in_specs,
        out_specs=out_specs,
        out_shape=out_shape,
        scratch_shapes=[
            pltpu.VMEM((ns, 1, SEQ, 128), BF16), pltpu.VMEM((ns, 1, 128, SEQ), BF16),
            pltpu.VMEM((ns, 1, SEQ, 256), BF16), pltpu.VMEM((ns, 1, 256, SEQ), BF16),
            pltpu.VMEM((ns, 3, 2, 1, SEQ, SEQ), F32),
            pltpu.VMEM((ns, 2 * BRANCH_W, SEQ), F32),
        ],
        input_output_aliases=aliases,
        compiler_params=_params("arbitrary"),
        name=f"ctx_mix_{layer}",
    )(*args)
    return outs[0], tuple(outs[1:])


def _lat_pre_call(layer, x, row0, mods, p, w):
    tm = LAT_PRE_TM
    tile0 = row0 // tm
    tiles_per_seq = DEC_SEQ // tm
    rope_spec = pl.BlockSpec((tm, LANES), lambda i: (i % tiles_per_seq, 0))
    rows = lambda w, dt: (pl.BlockSpec((tm, w), lambda i: (i, 0)),
                          jax.ShapeDtypeStruct((LAT_ROWS, w), dt))
    cols = lambda w: (pl.BlockSpec((None, w, tm), lambda i: (i // tiles_per_seq, 0, i % tiles_per_seq)),
                      jax.ShapeDtypeStruct((DEC_BATCH, w, DEC_SEQ), BF16))
    outs = [rows(256, BF16), rows(256, BF16), rows(128, BF16), cols(128),
            rows(256, BF16), rows(256, BF16), cols(256), rows(256, F32), rows(256, BF16)]
    return pl.pallas_call(
        _lat_pre_kernel,
        grid=(LAT_ROWS // tm,),
        in_specs=[
            pl.BlockSpec((tm, D_MODEL), lambda i: (tile0 + i, 0)),
            _mod_spec(lambda i: 1 + i // tiles_per_seq),
            _layer_block((1, D_MODEL), layer),
            _resident((D_MODEL, IN_W)),
            _layer_block((1, 256), layer),
            _layer_block((1, 128), layer),
            _layer_block((1, 256), layer),
            rope_spec, rope_spec, rope_spec, rope_spec,
        ],
        out_specs=[spec for spec, _ in outs],
        out_shape=[shape for _, shape in outs],
        compiler_params=_params("arbitrary"),
        name=f"lat_pre_{layer}",
    )(x, mods, p["norm1_g"], w["w_in"], p["gq"], p["gk"], p["gsgu"],
      p["rope_cb"], p["rope_sb"], p["rope_cc"], p["rope_sc"])


def _lat_mix_call(layer, pre, caches, p):
    tq = LAT_TQ
    nq = DEC_SEQ // tq
    n_chunks = KEYS_LAT // KEY_CHUNK
    seq_block = lambda w: pl.BlockSpec((DEC_SEQ, w), lambda b, q: (b, 0))
    seq_block_t = lambda w: pl.BlockSpec((None, w, DEC_SEQ), lambda b, q: (b, 0, 0))
    q_block = lambda w: pl.BlockSpec((tq, w), lambda b, q: (b * nq + q, 0))
    cache_block = lambda w: pl.BlockSpec((None, None, PAST_LEN, w), lambda b, q: (b, layer, 0, 0))
    return pl.pallas_call(
        functools.partial(_lat_mix_kernel, layer),
        grid=(DEC_BATCH, nq),
        in_specs=[
            seq_block(256), q_block(256), seq_block(128), seq_block_t(128),
            q_block(256), seq_block(256), seq_block_t(256),
            q_block(256), q_block(256),
            cache_block(128), cache_block(128), cache_block(256), cache_block(256),
            _layer_block((1, 256), layer),
            _layer_block((4, DIFF_QK_DIM), layer),
            pl.BlockSpec((tq, 2 * DEC_SEQ), lambda b, q: (q, 0)),
            _full((BRANCH_W, BRANCH_W)),
            _full((BRANCH_W, BRANCH_W)),
            _layer_block((BRANCH_W, BRANCH_W), layer),
            _layer_block((SGU_GROUPS * SGU_CHUNK, SGU_CHUNK), layer),
            _layer_block((SGU_CHUNK, BRANCH_W), layer),
        ],
        out_specs=q_block(D_MODEL),
        out_shape=jax.ShapeDtypeStruct((LAT_ROWS, D_MODEL), BF16),
        scratch_shapes=[
            pltpu.VMEM((n_chunks, KEY_CHUNK, 128), BF16), pltpu.VMEM((n_chunks, 128, KEY_CHUNK), BF16),
            pltpu.VMEM((n_chunks, KEY_CHUNK, 256), BF16), pltpu.VMEM((n_chunks, 256, KEY_CHUNK), BF16),
            pltpu.VMEM((2 * DEC_SEQ, BRANCH_W), BF16),
            pltpu.VMEM((3, 2, n_chunks, KEY_CHUNK, tq), F32),
            pltpu.VMEM((2 * BRANCH_W, tq), F32),
        ],
        compiler_params=_params("arbitrary", "arbitrary"),
        name=f"lat_mix_{layer}",
    )(*pre, *caches, p["gdiff"], p["lamv"],
      p["dft_lat"], p["bdc"], p["bds"], p["w_fourier"], p["wsp"], p["bsp"])


def _merge_call(layer, x_ctx, x_lat, lat_row0, br_ctx, br_lat, mods, p, w, casts):
    tm = ROW_TM
    ctx_tiles = CTX_ROWS // tm
    lat_tile0 = lat_row0 // tm
    ctx_tile = lambda i: (jnp.minimum(i, ctx_tiles - 1), 0)
    jobs = [_cast_job(c, layer + 1) for c in casts]
    outs = pl.pallas_call(
        functools.partial(_merge_kernel, len(jobs)),
        grid=(ALL_ROWS // tm,),
        in_specs=[
            pl.BlockSpec((tm, D_MODEL), ctx_tile),
            pl.BlockSpec((tm, D_MODEL), lambda i: (lat_tile0 + jnp.maximum(i - ctx_tiles, 0), 0)),
            pl.BlockSpec((tm, D_MODEL), ctx_tile),
            pl.BlockSpec((tm, D_MODEL), lambda i: (jnp.maximum(i - ctx_tiles, 0), 0)),
            _mod_spec(_row_mod_select(tm)),
            _layer_block((1, D_MODEL), layer),
            _resident((D_MODEL, N_BRANCH * D_MODEL)),
            _resident((N_BRANCH * BRANCH_W, D_MODEL)),
            _resident((D_MODEL, D_MODEL)),
        ] + [j[0] for j in jobs],
        out_specs=[pl.BlockSpec((tm, D_MODEL), lambda i: (i, 0))] + [j[1] for j in jobs],
        out_shape=[jax.ShapeDtypeStruct((ALL_ROWS, D_MODEL), F32)] + [j[2] for j in jobs],
        compiler_params=_params("arbitrary"),
        name=f"merge_{layer}",
    )(x_ctx, x_lat, br_ctx, br_lat, mods, p["norm1_g"], w["w_gate"], w["w_branch"], w["w_out"], *casts)
    return outs[0], outs[1:]


def _mlp_call(layer, x, mods, p, w, casts=(), ada=None, final_g=None):
    tm = ROW_TM
    steps = ALL_ROWS // tm
    ctx_tiles = CTX_ROWS // tm
    in_specs = [
        pl.BlockSpec((tm, D_MODEL), lambda i: (i, 0)),
        _mod_spec(_row_mod_select(tm)),
        _layer_block((1, D_MODEL), layer),
        _resident((D_MODEL, D_FF)),
        _resident((D_FF, D_MODEL)),
    ]
    args = [x, mods, p["norm2_g"], w["w_mlp1"], w["w_mlp2"]]
    if final_g is None:
        ada_w = N_MOD * D_MODEL // steps
        jobs = [_cast_job(c, layer + 1) for c in casts]
        body = functools.partial(_mlp_kernel, len(jobs))
        in_specs += [
            _full((COND_ROWS, D_MODEL)),
            pl.BlockSpec((None, D_MODEL, ada_w), lambda i: (layer + 1, 0, i)),
            pl.BlockSpec((None, 1, ada_w), lambda i: (layer + 1, 0, i)),
        ] + [j[0] for j in jobs]
        args += list(ada) + list(casts)
        out_specs = [pl.BlockSpec((tm, D_MODEL), lambda i: (i, 0)),
                     pl.BlockSpec((COND_ROWS, ada_w), lambda i: (0, i))] + [j[1] for j in jobs]
        out_shape = [jax.ShapeDtypeStruct((ALL_ROWS, D_MODEL), F32),
                     jax.ShapeDtypeStruct((COND_ROWS, N_MOD * D_MODEL), F32)] + [j[2] for j in jobs]
    else:
        body = _mlp_final_kernel
        in_specs.append(_full((1, D_MODEL)))
        args.append(final_g)
        out_specs = [
            pl.BlockSpec((tm, D_MODEL), lambda i: (jnp.minimum(i, ctx_tiles - 1), 0)),
            pl.BlockSpec((tm, D_MODEL), lambda i: (jnp.maximum(i - ctx_tiles, 0), 0)),
        ]
        out_shape = [jax.ShapeDtypeStruct((CTX_ROWS, D_MODEL), F32),
                     jax.ShapeDtypeStruct((LAT_ROWS, D_MODEL), F32)]
    return pl.pallas_call(
        body,
        grid=(steps,),
        in_specs=in_specs,
        out_specs=out_specs,
        out_shape=out_shape,
        compiler_params=_params("arbitrary"),
        name=f"mlp_{layer}",
    )(*args)


def kernel(x_prompt, x_sample, c, cache_gqa_k, cache_gqa_v, cache_diff_k, cache_diff_v, c_ctx, w_ada, b_ada, norm1_g, norm2_g, w_in, w_fourier, q_norm_g, k_norm_g, lambda_q1, lambda_k1, lambda_q2, lambda_k2, diff_norm_g, sgu_norm_g, w_spatial, b_spatial, w_gate, w_branch, w_out, w_mlp1, w_mlp2, final_norm_g):
    bdc, bds = _channel_dft_tables()
    cb, sb = _rope_tables(DEC_SEQ, GQA_HEAD_DIM)
    cc, sc = _rope_tables(DEC_SEQ, DIFF_QK_DIM)
    p = {
        "norm1_g": norm1_g.reshape(DEPTH, 1, D_MODEL),
        "norm2_g": norm2_g.reshape(DEPTH, 1, D_MODEL),
        "w_fourier": w_fourier.astype(BF16),
        "gq": jnp.tile(q_norm_g, (1, GQA_HEADS)).reshape(DEPTH, 1, 256),
        "gk": jnp.tile(k_norm_g, (1, GQA_KV_HEADS)).reshape(DEPTH, 1, 128),
        "gsgu": sgu_norm_g.reshape(DEPTH, 1, BRANCH_W),
        "gdiff": jnp.tile(diff_norm_g, (1, DIFF_HEADS)).reshape(DEPTH, 1, 256),
        "lamv": jnp.stack([lambda_q1, lambda_k1, lambda_q2, lambda_k2], axis=1),
        "wsp": w_spatial.reshape(DEPTH, SGU_GROUPS * SGU_CHUNK, SGU_CHUNK).astype(BF16),
        "bsp": jnp.repeat(jnp.swapaxes(b_spatial, 1, 2), BRANCH_W // SGU_GROUPS, axis=2),
        "dft_ctx": jnp.asarray(_dft_tables(SEQ)).astype(BF16),
        "dft_lat": jnp.asarray(_dft_tables(DEC_SEQ)).astype(BF16),
        "bdc": jnp.asarray(bdc).astype(BF16),
        "bds": jnp.asarray(bds).astype(BF16),
        "rope_cb": jnp.asarray(cb), "rope_sb": jnp.asarray(sb),
        "rope_cc": jnp.asarray(cc), "rope_sc": jnp.asarray(sc),
    }
    lat_caches = (
        cache_gqa_k.reshape(DEC_BATCH, DEPTH, PAST_LEN, 128),
        cache_gqa_v.reshape(DEC_BATCH, DEPTH, PAST_LEN, 128),
        cache_diff_k.reshape(DEC_BATCH, DEPTH, PAST_LEN, 256),
        cache_diff_v.reshape(DEC_BATCH, DEPTH, PAST_LEN, 256),
    )

    cond = jnp.concatenate(
        [c_ctx[None, :], c, jnp.zeros((COND_ROWS - 1 - DEC_BATCH, D_MODEL), F32)], axis=0)
    ada = (cond, w_ada, b_ada.reshape(DEPTH, 1, N_MOD * D_MODEL))
    mods = _ada_call(*ada).reshape(COND_ROWS, N_MOD, D_MODEL)

    merge_casts = {"w_in": w_in, "w_gate": w_gate}
    mlp_casts = {"w_branch": w_branch.reshape(DEPTH, N_BRANCH * BRANCH_W, D_MODEL), "w_out": w_out,
                 "w_mlp1": w_mlp1, "w_mlp2": w_mlp2}
    w = {k: v[0].astype(BF16) for k, v in {**merge_casts, **mlp_casts}.items()}

    x_ctx, x_lat, lat_row0 = x_prompt.reshape(CTX_ROWS, D_MODEL), x_sample.reshape(LAT_ROWS, D_MODEL), 0
    new_caches = None
    for layer in range(DEPTH):
        last = layer + 1 == DEPTH
        br_ctx, new_caches = _ctx_call(layer, x_ctx, mods, p, w, new_caches)
        pre = _lat_pre_call(layer, x_lat, lat_row0, mods, p, w)
        br_lat = _lat_mix_call(layer, pre, lat_caches, p)
        x, cast_a = _merge_call(layer, x_ctx, x_lat, lat_row0, br_ctx, br_lat, mods, p, w,
                                [] if last else list(merge_casts.values()))
        if last:
            y_ctx, y_lat = _mlp_call(layer, x, mods, p, w, final_g=final_norm_g.reshape(1, D_MODEL))
        else:
            outs = _mlp_call(layer, x, mods, p, w, list(mlp_casts.values()), ada)
            x, mods, cast_b = outs[0], outs[1].reshape(COND_ROWS, N_MOD, D_MODEL), outs[2:]
            w = dict(zip(list(merge_casts) + list(mlp_casts), list(cast_a) + list(cast_b)))
            x_ctx, x_lat, lat_row0 = x, x, CTX_ROWS

    gk, gv, dk, dv = new_caches
    return (
        y_ctx.reshape(BATCH, SEQ, D_MODEL),
        y_lat.reshape(DEC_BATCH, DEC_SEQ, D_MODEL),
        gk.reshape(BATCH, DEPTH, SEQ, GQA_KV_HEADS, GQA_HEAD_DIM),
        gv.reshape(BATCH, DEPTH, SEQ, GQA_KV_HEADS, GQA_HEAD_DIM),
        dk.reshape(BATCH, DEPTH, SEQ, DIFF_HEADS, 2, DIFF_QK_DIM),
        dv.reshape(BATCH, DEPTH, SEQ, DIFF_HEADS, DIFF_V_DIM),
    )
```

```python
import functools
import math

import numpy as np
import jax
import jax.numpy as jnp
from jax import lax
from jax.experimental import pallas as pl
from jax.experimental.pallas import tpu as pltpu

F32 = jnp.float32
BF16 = jnp.bfloat16

D_MODEL = 1024
BATCH = 32
SEQ = 256
DEPTH = 4
DEC_BATCH = 2
DEC_SEQ = 2048
PAST_LEN = 512
GRID_W = 64
ROPE_THETA = 10000.0
EPS = 1e-6
N_BRANCH = 4
BRANCH_W = 256
FNET_GW = 64
GQA_HEADS = 4
GQA_KV_HEADS = 2
GQA_HEAD_DIM = 64
DIFF_HEADS = 4
DIFF_V_DIM = 64
DIFF_QK_DIM = 32
SGU_GROUPS = 4
SGU_CHUNK = 128
D_FF = 4 * D_MODEL
N_MOD = 6
IN_W = 2048

CTX_ROWS = BATCH * SEQ
LAT_ROWS = DEC_BATCH * DEC_SEQ
ALL_ROWS = CTX_ROWS + LAT_ROWS
KEYS_LAT = PAST_LEN + DEC_SEQ

LANES = 128
VMEM_LIMIT = 52 * 1024 * 1024

CTX_SEQS_PER_STEP = 4
LAT_PRE_TM = 512
LAT_TQ = 256
KEY_CHUNK = 256
ROW_TM = 512
COND_ROWS = 8
CAST_STEPS = 16

_A0, _QB0, _KB0, _VB0, _QC0, _KC0, _VC0, _DU0, _DV0 = 0, 256, 512, 640, 768, 1024, 1280, 1536, 1792
_Q_HEAD_ORDER = (0, 2, 1, 3)
_LOG2E = math.log2(math.e)
_GQA_Q_SCALE = _LOG2E / math.sqrt(GQA_HEAD_DIM)
_DIFF_Q_SCALE = _LOG2E / math.sqrt(DIFF_QK_DIM)


def _dft_tables(length):
    k = np.arange(length, dtype=np.int64)
    idx = (k[:, None] * k[None, :]) % length
    ang = 2.0 * np.pi * idx.astype(np.float64) / length
    return np.concatenate([np.cos(ang), -np.sin(ang)], axis=1).astype(np.float32)


def _channel_dft_tables():
    k = np.arange(FNET_GW, dtype=np.int64)
    idx = (k[:, None] * k[None, :]) % FNET_GW
    ang = 2.0 * np.pi * idx.astype(np.float64) / FNET_GW
    c = np.cos(ang) / math.sqrt(FNET_GW)
    s = np.sin(ang) / math.sqrt(FNET_GW)
    eye = np.eye(BRANCH_W // FNET_GW)
    return (np.kron(eye, c).astype(np.float32), np.kron(eye, s).astype(np.float32))


def _rope_tables(length, dim):
    rows = length // GRID_W
    row = np.repeat(np.arange(rows, dtype=np.float64), GRID_W)
    col = np.tile(np.arange(GRID_W, dtype=np.float64), rows)
    quarter = dim // 4
    inv = ROPE_THETA ** (-np.arange(quarter, dtype=np.float64) / quarter)
    ang = np.concatenate([row[:, None] * inv, col[:, None] * inv], axis=-1)
    c = np.concatenate([np.cos(ang), np.cos(ang)], axis=-1)
    s = np.concatenate([-np.sin(ang), np.sin(ang)], axis=-1)
    reps = LANES // dim
    return (np.tile(c, (1, reps)).astype(np.float32), np.tile(s, (1, reps)).astype(np.float32))


def _dot(a, b):
    return jnp.dot(a, b, preferred_element_type=F32)


def _dot_nt(a, b):
    return lax.dot_general(a, b, (((1,), (1,)), ((), ())), preferred_element_type=F32)


def _rms_mod(x, g, scale, shift, out_scale=1.0):
    y = x * lax.rsqrt(jnp.mean(x * x, axis=-1, keepdims=True) + EPS)
    return (y * (g * (1.0 + scale) * out_scale) + shift * out_scale).astype(BF16)


def _gelu(x):
    c = math.sqrt(2.0 / math.pi)
    return 0.5 * x * (1.0 + jnp.tanh(c * (x + 0.044715 * (x * x * x))))


def _group_ones(width, group):
    r = lax.broadcasted_iota(jnp.int32, (width, width), 0) // group
    c = lax.broadcasted_iota(jnp.int32, (width, width), 1) // group
    return (r == c).astype(BF16)


def _group_mean_sq(x, group):
    x2 = x * x
    hi = x2.astype(BF16)
    lo = (x2 - hi.astype(F32)).astype(BF16)
    ones = _group_ones(x.shape[-1], group)
    return (_dot(hi, ones) + _dot(lo, ones)) * (1.0 / group)


def _lane_mask(lo, hi):
    lane = lax.broadcasted_iota(jnp.int32, (1, LANES), 1)
    return (lane >= lo) & (lane < hi)


def _swap_halves(x, dim):
    n = x.shape[-1]
    half = dim // 2
    lane = lax.broadcasted_iota(jnp.int32, (1, n), 1)
    fwd = pltpu.roll(x, n - half, 1)
    bwd = pltpu.roll(x, half, 1)
    return jnp.where((lane % dim) < half, fwd, bwd)


def _rope(x, cos, sin, dim):
    reps = x.shape[-1] // LANES
    if reps > 1:
        cos = jnp.concatenate([cos] * reps, axis=1)
        sin = jnp.concatenate([sin] * reps, axis=1)
    return x * cos + _swap_halves(x, dim) * sin


def _pair_q_heads(q):
    b0, b1 = q[:, :LANES], q[:, LANES:]
    lo = _lane_mask(0, GQA_HEAD_DIM)
    r0 = pltpu.roll(b0, GQA_HEAD_DIM, 1)
    r1 = pltpu.roll(b1, GQA_HEAD_DIM, 1)
    return jnp.concatenate([jnp.where(lo, b0, r1), jnp.where(lo, r0, b1)], axis=1)


def _fold8(x, op):
    return op(x.reshape(x.shape[0] // 8, 8, x.shape[1]), axis=0)


class _Unit:
    def __init__(self, k_ref, k_lanes, q_blk, masks, vt_ref, v_rows, out_row0, is_diff):
        self.k_ref, self.k_lanes, self.vt_ref, self.v_rows = k_ref, k_lanes, vt_ref, v_rows
        self.out_row0, self.is_diff = out_row0, is_diff
        self.qm = [jnp.where(_lane_mask(lo, hi), q_blk, jnp.zeros_like(q_blk)) for lo, hi in masks]
        self.m = self.scale = None


def _make_units(qb, qc, kb_ref, kc_ref, vbt_ref, vct_ref):
    gqa_units, diff_units = [], []
    for pos, head in enumerate(_Q_HEAD_ORDER):
        blk, kv = pos // 2, pos % 2
        rows = slice(kv * GQA_HEAD_DIM, (kv + 1) * GQA_HEAD_DIM)
        gqa_units.append(_Unit(kb_ref, slice(0, LANES), qb[:, blk * LANES:(blk + 1) * LANES],
                               [(rows.start, rows.stop)], vbt_ref, rows, head * GQA_HEAD_DIM, False))
    for head in range(DIFF_HEADS):
        blk, base = head // 2, (head % 2) * DIFF_V_DIM
        lanes = slice(blk * LANES, (blk + 1) * LANES)
        diff_units.append(_Unit(kc_ref, lanes, qc[:, lanes],
                                [(base, base + DIFF_QK_DIM), (base + DIFF_QK_DIM, base + DIFF_V_DIM)],
                                vct_ref, slice(head * DIFF_V_DIM, (head + 1) * DIFF_V_DIM),
                                BRANCH_W + head * DIFF_V_DIM, True))
    return [u for pair in zip(gqa_units, diff_units) for u in pair]


def _attention_pipeline(units, lam, s_buf, ot_ref):
    n = len(units)
    tq = s_buf.shape[-1]
    for j in range(n + 2):
        ua = units[j] if j < n else None
        ub = units[j - 1] if 1 <= j <= n else None
        uc = units[j - 2] if 2 <= j <= n + 1 and units[j - 2].is_diff else None
        sa, sb, sc = j % 3, (j - 1) % 3, (j - 2) % 3
        init = {}
        if ua is not None:
            init["mx"] = [jnp.full((8, tq), -jnp.inf, F32) for _ in ua.qm]
        if ub is not None:
            init["l"] = [jnp.zeros((8, tq), F32) for _ in ub.qm]
            if not ub.is_diff:
                init["ob"] = jnp.zeros((DIFF_V_DIM, tq), F32)
        if uc is not None:
            init["oc"] = jnp.zeros((DIFF_V_DIM, tq), F32)

        def body(c, carry, ua=ua, ub=ub, uc=uc, sa=sa, sb=sb, sc=sc):
            new = dict(carry)
            if ua is not None:
                k_c = ua.k_ref[c, :, ua.k_lanes]
                mx = []
                for mi, qm in enumerate(ua.qm):
                    s = _dot_nt(k_c, qm)
                    s_buf[sa, mi, c] = s
                    mx.append(jnp.maximum(carry["mx"][mi], _fold8(s, jnp.max)))
                new["mx"] = mx
            if ub is not None:
                ls = []
                for mi in range(len(ub.qm)):
                    e = jnp.exp2(s_buf[sb, mi, c] - ub.m[mi])
                    ls.append(carry["l"][mi] + _fold8(e, jnp.sum))
                    if ub.is_diff:
                        s_buf[sb, mi, c] = e
                    else:
                        new["ob"] = carry["ob"] + _dot(ub.vt_ref[c, ub.v_rows, :], e.astype(BF16))
                new["l"] = ls
            if uc is not None:
                w = s_buf[sc, 0, c] - s_buf[sc, 1, c] * uc.scale[1]
                new["oc"] = carry["oc"] + _dot(uc.vt_ref[c, uc.v_rows, :], w.astype(BF16))
            return new

        out = lax.fori_loop(0, s_buf.shape[2], body, init, unroll=True)
        if ua is not None:
            ua.m = [jnp.max(mx, axis=0, keepdims=True) for mx in out["mx"]]
        if ub is not None:
            l = [jnp.sum(x, axis=0, keepdims=True) for x in out["l"]]
            if ub.is_diff:
                ub.scale = [1.0 / l[0], lam * l[0] / l[1]]
            else:
                ot_ref[ub.out_row0:ub.out_row0 + DIFF_V_DIM, :] = out["ob"] * (1.0 / l[0])
        if uc is not None:
            o = out["oc"] * uc.scale[0]
            ot_ref[uc.out_row0:uc.out_row0 + DIFF_V_DIM, :] = (
                o * lax.rsqrt(jnp.mean(o * o, axis=0, keepdims=True) + EPS))


def _lambda(lamv, layer):
    lam_init = 0.8 - 0.6 * math.exp(-0.3 * layer)
    a = jnp.sum(lamv[0:1] * lamv[1:2], axis=-1, keepdims=True)
    b = jnp.sum(lamv[2:3] * lamv[3:4], axis=-1, keepdims=True)
    return jnp.exp(a) - jnp.exp(b) + lam_init, 1.0 - lam_init


def _sgu(du, dvn, wsp, bias):
    lane = lax.broadcasted_iota(jnp.int32, (1, BRANCH_W), 1)
    outs = []
    for n in range(du.shape[0] // SGU_CHUNK):
        rows = slice(n * SGU_CHUNK, (n + 1) * SGU_CHUNK)
        r = _dot(wsp, dvn[rows].astype(BF16))
        s = bias
        for g in range(SGU_GROUPS):
            gm = (lane >= g * 64) & (lane < (g + 1) * 64)
            s = s + jnp.where(gm, r[g * SGU_CHUNK:(g + 1) * SGU_CHUNK], 0.0)
        outs.append(du[rows] * s)
    return outs[0] if len(outs) == 1 else jnp.concatenate(outs, axis=0)


def _fourier_stage1(a, bdc, bds):
    return jnp.concatenate([_dot(a, bdc), _dot(a, bds)], axis=0).astype(BF16)


def _fourier_stage2(dft_rows, f, length, w_fourier):
    y = _dot(dft_rows, f) * (1.0 / math.sqrt(length))
    return _dot(y.astype(BF16), w_fourier)


def _ada_kernel(cond_ref, w_ref, b_ref, o_ref):
    c = cond_ref[...]
    s = c * jax.nn.sigmoid(c)
    o_ref[...] = _dot(s.astype(BF16), w_ref[...].astype(BF16)) + b_ref[...]


def _ctx_kernel(layer, x_ref, mod_ref, g1_ref, win_ref, gq_ref, gk_ref, gsgu_ref, gdiff_ref,
                lamv_ref, dft_ref, bdc_ref, bds_ref, wf_ref, wsp_ref, bsp_ref,
                br_ref, ok_ref, ov_ref, odk_ref, odv_ref,
                kb_s, vbt_s, kc_s, vct_s, s_buf, ot_s):
    x = x_ref[...]
    mod = mod_ref[...]
    h = _rms_mod(x, g1_ref[...], mod[1:2], mod[0:1])
    proj = _dot(h, win_ref[...])
    qb = proj[:, _QB0:_KB0]
    kb = proj[:, _KB0:_VB0]
    qb = _pair_q_heads(qb * lax.rsqrt(_group_mean_sq(qb, GQA_HEAD_DIM) + EPS) * gq_ref[...])
    kb = kb * lax.rsqrt(_group_mean_sq(kb, GQA_HEAD_DIM) + EPS) * gk_ref[...]
    vb = proj[:, _VB0:_QC0]
    qc = proj[:, _QC0:_KC0]
    kc = proj[:, _KC0:_VC0]
    vc = proj[:, _VC0:_DU0]
    du = _gelu(proj[:, _DU0:_DV0])
    dv = _gelu(proj[:, _DV0:IN_W])
    dvn = dv * lax.rsqrt(jnp.mean(dv * dv, axis=-1, keepdims=True) + EPS) * gsgu_ref[...]
    lam, lam_scale = _lambda(lamv_ref[...], layer)
    qb = (qb * _GQA_Q_SCALE).astype(BF16)
    qc = (qc * _DIFF_Q_SCALE).astype(BF16)
    a = proj[:, _A0:_QB0].astype(BF16)
    for s in range(CTX_SEQS_PER_STEP):
        rows = slice(s * SEQ, (s + 1) * SEQ)
        ok_ref[s] = kb[rows]
        ov_ref[s] = vb[rows]
        odk_ref[s] = kc[rows]
        odv_ref[s] = vc[rows]
        kb_s[s, 0] = kb[rows].astype(BF16)
        kc_s[s, 0] = kc[rows].astype(BF16)
        vbt_s[s, 0] = vb[rows].T.astype(BF16)
        vct_s[s, 0] = vc[rows].T.astype(BF16)
        f = _fourier_stage1(a[rows], bdc_ref[...], bds_ref[...])
        br_ref[rows, 0:256] = _fourier_stage2(dft_ref[...], f, SEQ, wf_ref[...]).astype(BF16)
        br_ref[rows, 768:1024] = _sgu(du[rows], dvn[rows], wsp_ref[...], bsp_ref[...]).astype(BF16)
    for s in range(CTX_SEQS_PER_STEP):
        rows = slice(s * SEQ, (s + 1) * SEQ)
        units = _make_units(qb[rows], qc[rows], kb_s.at[s], kc_s.at[s], vbt_s.at[s], vct_s.at[s])
        _attention_pipeline(units, lam, s_buf.at[s], ot_s.at[s])
        br_ref[rows, 256:512] = ot_s[s, 0:BRANCH_W, :].T.astype(BF16)
        br_ref[rows, 512:768] = (ot_s[s, BRANCH_W:2 * BRANCH_W, :].T
                                 * (gdiff_ref[...] * lam_scale)).astype(BF16)


def _lat_pre_kernel(x_ref, mod_ref, g1_ref, win_ref, gq_ref, gk_ref, gsgu_ref,
                    cb_ref, sb_ref, cc_ref, sc_ref,
                    a_ref, qb_ref, kb_ref, vbt_ref, qc_ref, kc_ref, vct_ref, du_ref, dvn_ref):
    x = x_ref[...]
    mod = mod_ref[...]
    h = _rms_mod(x, g1_ref[...], mod[1:2], mod[0:1])
    cb, sb, cc, sc = cb_ref[...], sb_ref[...], cc_ref[...], sc_ref[...]
    proj = lambda c0, c1: _dot(h, win_ref[:, c0:c1])
    qb = proj(_QB0, _KB0)
    kvb = proj(_KB0, _QC0)
    kb, vb = kvb[:, :LANES], kvb[:, LANES:]
    qb = qb * lax.rsqrt(_group_mean_sq(qb, GQA_HEAD_DIM) + EPS) * gq_ref[...]
    kb = kb * lax.rsqrt(_group_mean_sq(kb, GQA_HEAD_DIM) + EPS) * gk_ref[...]
    qb_ref[...] = _pair_q_heads(_rope(qb, cb, sb, GQA_HEAD_DIM) * _GQA_Q_SCALE).astype(BF16)
    kb_ref[...] = _rope(kb, cb, sb, GQA_HEAD_DIM).astype(BF16)
    vbt_ref[...] = vb.T.astype(BF16)
    qc = proj(_QC0, _KC0)
    kc = proj(_KC0, _VC0)
    qc_ref[...] = (_rope(qc, cc, sc, DIFF_QK_DIM) * _DIFF_Q_SCALE).astype(BF16)
    kc_ref[...] = _rope(kc, cc, sc, DIFF_QK_DIM).astype(BF16)
    vc = proj(_VC0, _DU0)
    du = proj(_DU0, _DV0)
    vct_ref[...] = vc.T.astype(BF16)
    du_ref[...] = _gelu(du)
    dv = _gelu(proj(_DV0, IN_W))
    a_ref[...] = proj(_A0, _QB0).astype(BF16)
    dvn_ref[...] = (dv * lax.rsqrt(jnp.mean(dv * dv, axis=-1, keepdims=True) + EPS)
                    * gsgu_ref[...]).astype(BF16)


def _lat_mix_kernel(layer, a_ref, qb_ref, kb_ref, vbt_ref, qc_ref, kc_ref, vct_ref,
                    du_ref, dvn_ref, ck_ref, cv_ref, cdk_ref, cdv_ref, gdiff_ref, lamv_ref,
                    dft_ref, bdc_ref, bds_ref, wf_ref, wsp_ref, bsp_ref,
                    br_ref, kb_s, vbt_s, kc_s, vct_s, f_s, s_buf, ot_s):
    kc = KEY_CHUNK
    past_chunks = PAST_LEN // kc

    @pl.when(pl.program_id(1) == 0)
    def _fill():
        for c in range(KEYS_LAT // kc):
            if c < past_chunks:
                rows = slice(c * kc, (c + 1) * kc)
                kb_s[c] = ck_ref[rows, :].astype(BF16)
                kc_s[c] = cdk_ref[rows, :].astype(BF16)
                vbt_s[c] = cv_ref[rows, :].T.astype(BF16)
                vct_s[c] = cdv_ref[rows, :].T.astype(BF16)
            else:
                rows = slice((c - past_chunks) * kc, (c - past_chunks + 1) * kc)
                kb_s[c] = kb_ref[rows, :]
                kc_s[c] = kc_ref[rows, :]
                vbt_s[c] = vbt_ref[:, rows]
                vct_s[c] = vct_ref[:, rows]
        f_s[...] = _fourier_stage1(a_ref[...], bdc_ref[...], bds_ref[...])

    lam, lam_scale = _lambda(lamv_ref[...], layer)
    units = _make_units(qb_ref[...], qc_ref[...], kb_s, kc_s, vbt_s, vct_s)
    _attention_pipeline(units, lam, s_buf, ot_s)
    br_ref[:, 0:256] = _fourier_stage2(dft_ref[...], f_s[...], DEC_SEQ, wf_ref[...]).astype(BF16)
    br_ref[:, 768:1024] = _sgu(du_ref[...], dvn_ref[...].astype(F32), wsp_ref[...],
                               bsp_ref[...]).astype(BF16)
    br_ref[:, 256:512] = ot_s[0:BRANCH_W, :].T.astype(BF16)
    br_ref[:, 512:768] = (ot_s[BRANCH_W:2 * BRANCH_W, :].T * (gdiff_ref[...] * lam_scale)).astype(BF16)


def _cast_slabs(n_cast, refs):
    @pl.when(pl.program_id(0) < CAST_STEPS)
    def _():
        for src, dst in zip(refs[:n_cast], refs[n_cast:]):
            dst[...] = src[...].astype(BF16)


def _merge_kernel(n_cast, xc_ref, xl_ref, brc_ref, brl_ref, mod_ref, g1_ref, wg_ref, wb_ref, wo_ref,
                  *rest):
    cast_src, o_ref, cast_dst = rest[:n_cast], rest[n_cast], rest[n_cast + 1:]
    is_ctx = pl.program_id(0) < CTX_ROWS // ROW_TM
    x = jnp.where(is_ctx, xc_ref[...], xl_ref[...])
    br = jnp.where(is_ctx, brc_ref[...], brl_ref[...])
    mod = mod_ref[...]
    h_half = _rms_mod(x, g1_ref[...], mod[1:2], mod[0:1], out_scale=0.5)
    br_half = br * 0.5
    merged = None
    for n in range(N_BRANCH):
        t = jnp.tanh(_dot(h_half, wg_ref[:, n * D_MODEL:(n + 1) * D_MODEL]))
        term = (t + 1.0) * _dot(br_half[:, n * BRANCH_W:(n + 1) * BRANCH_W],
                                wb_ref[n * BRANCH_W:(n + 1) * BRANCH_W, :])
        merged = term if merged is None else merged + term
    o_ref[...] = x + mod[2:3] * _dot(merged.astype(BF16), wo_ref[...])
    _cast_slabs(n_cast, cast_src + cast_dst)


def _mlp_block(x, mod, g2, w1_ref, w2_ref):
    h = _rms_mod(x, g2, mod[4:5], mod[3:4])
    acc = None
    for n in range(D_FF // D_MODEL):
        cols = slice(n * D_MODEL, (n + 1) * D_MODEL)
        hid = jnp.maximum(_dot(h, w1_ref[:, cols]), 0.0)
        term = _dot((hid * hid).astype(BF16), w2_ref[cols, :])
        acc = term if acc is None else acc + term
    return x + mod[5:6] * acc


def _mlp_kernel(n_cast, x_ref, mod_ref, g2_ref, w1_ref, w2_ref, cond_ref, wada_ref, bada_ref, *rest):
    cast_src, o_ref, modn_ref, cast_dst = rest[:n_cast], rest[n_cast], rest[n_cast + 1], rest[n_cast + 2:]
    o_ref[...] = _mlp_block(x_ref[...], mod_ref[...], g2_ref[...], w1_ref, w2_ref)
    _ada_kernel(cond_ref, wada_ref, bada_ref, modn_ref)
    _cast_slabs(n_cast, cast_src + cast_dst)


def _mlp_final_kernel(x_ref, mod_ref, g2_ref, w1_ref, w2_ref, gf_ref, yc_ref, yl_ref):
    x = _mlp_block(x_ref[...], mod_ref[...], g2_ref[...], w1_ref, w2_ref)
    y = x * lax.rsqrt(jnp.mean(x * x, axis=-1, keepdims=True) + EPS) * gf_ref[...]
    is_ctx = pl.program_id(0) < CTX_ROWS // ROW_TM

    @pl.when(is_ctx)
    def _():
        yc_ref[...] = y

    @pl.when(jnp.logical_not(is_ctx))
    def _():
        yl_ref[...] = y


def _params(*sem):
    return pltpu.CompilerParams(dimension_semantics=sem, vmem_limit_bytes=VMEM_LIMIT)


def _full(shape):
    nd = len(shape)
    return pl.BlockSpec(shape, lambda *_: (0,) * nd)


def _layer_block(shape, layer):
    nd = len(shape)
    return pl.BlockSpec((None,) + tuple(shape), lambda *_: (layer,) + (0,) * nd)


def _resident(shape):
    nd = len(shape)
    return pl.BlockSpec(shape, lambda *_: (0,) * nd, pipeline_mode=pl.Buffered(1))


def _cast_job(w, layer):
    _, r, c = w.shape
    slab = r // CAST_STEPS
    step = lambda i: jnp.minimum(i, CAST_STEPS - 1)
    return (pl.BlockSpec((None, slab, c), lambda i: (layer, step(i), 0)),
            pl.BlockSpec((slab, c), lambda i: (step(i), 0)),
            jax.ShapeDtypeStruct((r, c), BF16))


def _mod_spec(select):
    return pl.BlockSpec((None, N_MOD, D_MODEL), lambda *idx: (select(*idx), 0, 0))


def _row_mod_select(tm):
    ctx_tiles = CTX_ROWS // tm
    tiles_per_seq = DEC_SEQ // tm
    return lambda i: jnp.where(i < ctx_tiles, 0, 1 + (i - ctx_tiles) // tiles_per_seq)


def _ada_call(cond, w_ada, b_ada):
    return pl.pallas_call(
        _ada_kernel,
        grid=(N_MOD,),
        in_specs=[
            pl.BlockSpec((COND_ROWS, D_MODEL), lambda j: (0, 0)),
            pl.BlockSpec((None, D_MODEL, D_MODEL), lambda j: (0, 0, j)),
            pl.BlockSpec((None, 1, D_MODEL), lambda j: (0, 0, j)),
        ],
        out_specs=pl.BlockSpec((COND_ROWS, D_MODEL), lambda j: (0, j)),
        out_shape=jax.ShapeDtypeStruct((COND_ROWS, N_MOD * D_MODEL), F32),
        compiler_params=_params("arbitrary"),
        name="ada",
    )(cond, w_ada, b_ada)


def _ctx_call(layer, x, mods, p, w, caches):
    ns = CTX_SEQS_PER_STEP
    tm = ns * SEQ
    cache_shapes = [(BATCH, DEPTH, SEQ, 128), (BATCH, DEPTH, SEQ, 128),
                    (BATCH, DEPTH, SEQ, 256), (BATCH, DEPTH, SEQ, 256)]
    in_specs = [
        pl.BlockSpec((tm, D_MODEL), lambda i: (i, 0)),
        _mod_spec(lambda i: 0),
        _layer_block((1, D_MODEL), layer),
        _resident((D_MODEL, IN_W)),
        _layer_block((1, 256), layer),
        _layer_block((1, 128), layer),
        _layer_block((1, 256), layer),
        _layer_block((1, 256), layer),
        _layer_block((4, DIFF_QK_DIM), layer),
        _full((SEQ, 2 * SEQ)),
        _full((BRANCH_W, BRANCH_W)),
        _full((BRANCH_W, BRANCH_W)),
        _layer_block((BRANCH_W, BRANCH_W), layer),
        _layer_block((SGU_GROUPS * SGU_CHUNK, SGU_CHUNK), layer),
        _layer_block((SGU_CHUNK, BRANCH_W), layer),
    ]
    args = [x, mods, p["norm1_g"], w["w_in"], p["gq"], p["gk"], p["gsgu"], p["gdiff"], p["lamv"],
            p["dft_ctx"], p["bdc"], p["bds"], p["w_fourier"], p["wsp"], p["bsp"]]
    aliases = {}
    if caches is not None:
        for j, cache in enumerate(caches):
            in_specs.append(pl.BlockSpec(memory_space=pl.ANY))
            aliases[len(args)] = 1 + j
            args.append(cache)
    kernel = functools.partial(_ctx_kernel, layer)
    if caches is not None:
        base = kernel

        def kernel(*refs):
            n_in = 15
            return base(*refs[:n_in], *refs[n_in + 4:])
    out_specs = [pl.BlockSpec((tm, D_MODEL), lambda i: (i, 0))] + [
        pl.BlockSpec((ns, None, SEQ, s[-1]), lambda i: (i, layer, 0, 0)) for s in cache_shapes]
    out_shape = [jax.ShapeDtypeStruct((CTX_ROWS, D_MODEL), BF16)] + [
        jax.ShapeDtypeStruct(s, F32) for s in cache_shapes]
    outs = pl.pallas_call(
        kernel,
        grid=(BATCH // ns,),
        in_specs=in_specs,
        out_specs=out_specs,
        out_shape=out_shape,
        scratch_shapes=[
            pltpu.VMEM((ns, 1, SEQ, 128), BF16), pltpu.VMEM((ns, 1, 128, SEQ), BF16),
            pltpu.VMEM((ns, 1, SEQ, 256), BF16), pltpu.VMEM((ns, 1, 256, SEQ), BF16),
            pltpu.VMEM((ns, 3, 2, 1, SEQ, SEQ), F32),
            pltpu.VMEM((ns, 2 * BRANCH_W, SEQ), F32),
        ],
        input_output_aliases=aliases,
        compiler_params=_params("arbitrary"),
        name=f"ctx_mix_{layer}",
    )(*args)
    return outs[0], tuple(outs[1:])


def _lat_pre_call(layer, x, row0, mods, p, w):
    tm = LAT_PRE_TM
    tile0 = row0 // tm
    tiles_per_seq = DEC_SEQ // tm
    rope_spec = pl.BlockSpec((tm, LANES), lambda i: (i % tiles_per_seq, 0))
    rows = lambda w, dt: (pl.BlockSpec((tm, w), lambda i: (i, 0)),
                          jax.ShapeDtypeStruct((LAT_ROWS, w), dt))
    cols = lambda w: (pl.BlockSpec((None, w, tm), lambda i: (i // tiles_per_seq, 0, i % tiles_per_seq)),
                      jax.ShapeDtypeStruct((DEC_BATCH, w, DEC_SEQ), BF16))
    outs = [rows(256, BF16), rows(256, BF16), rows(128, BF16), cols(128),
            rows(256, BF16), rows(256, BF16), cols(256), rows(256, F32), rows(256, BF16)]
    return pl.pallas_call(
        _lat_pre_kernel,
        grid=(LAT_ROWS // tm,),
        in_specs=[
            pl.BlockSpec((tm, D_MODEL), lambda i: (tile0 + i, 0)),
            _mod_spec(lambda i: 1 + i // tiles_per_seq),
            _layer_block((1, D_MODEL), layer),
            _resident((D_MODEL, IN_W)),
            _layer_block((1, 256), layer),
            _layer_block((1, 128), layer),
            _layer_block((1, 256), layer),
            rope_spec, rope_spec, rope_spec, rope_spec,
        ],
        out_specs=[spec for spec, _ in outs],
        out_shape=[shape for _, shape in outs],
        compiler_params=_params("arbitrary"),
        name=f"lat_pre_{layer}",
    )(x, mods, p["norm1_g"], w["w_in"], p["gq"], p["gk"], p["gsgu"],
      p["rope_cb"], p["rope_sb"], p["rope_cc"], p["rope_sc"])


def _lat_mix_call(layer, pre, caches, p):
    tq = LAT_TQ
    nq = DEC_SEQ // tq
    n_chunks = KEYS_LAT // KEY_CHUNK
    seq_block = lambda w: pl.BlockSpec((DEC_SEQ, w), lambda b, q: (b, 0))
    seq_block_t = lambda w: pl.BlockSpec((None, w, DEC_SEQ), lambda b, q: (b, 0, 0))
    q_block = lambda w: pl.BlockSpec((tq, w), lambda b, q: (b * nq + q, 0))
    cache_block = lambda w: pl.BlockSpec((None, None, PAST_LEN, w), lambda b, q: (b, layer, 0, 0))
    return pl.pallas_call(
        functools.partial(_lat_mix_kernel, layer),
        grid=(DEC_BATCH, nq),
        in_specs=[
            seq_block(256), q_block(256), seq_block(128), seq_block_t(128),
            q_block(256), seq_block(256), seq_block_t(256),
            q_block(256), q_block(256),
            cache_block(128), cache_block(128), cache_block(256), cache_block(256),
            _layer_block((1, 256), layer),
            _layer_block((4, DIFF_QK_DIM), layer),
            pl.BlockSpec((tq, 2 * DEC_SEQ), lambda b, q: (q, 0)),
            _full((BRANCH_W, BRANCH_W)),
            _full((BRANCH_W, BRANCH_W)),
            _layer_block((BRANCH_W, BRANCH_W), layer),
            _layer_block((SGU_GROUPS * SGU_CHUNK, SGU_CHUNK), layer),
            _layer_block((SGU_CHUNK, BRANCH_W), layer),
        ],
        out_specs=q_block(D_MODEL),
        out_shape=jax.ShapeDtypeStruct((LAT_ROWS, D_MODEL), BF16),
        scratch_shapes=[
            pltpu.VMEM((n_chunks, KEY_CHUNK, 128), BF16), pltpu.VMEM((n_chunks, 128, KEY_CHUNK), BF16),
            pltpu.VMEM((n_chunks, KEY_CHUNK, 256), BF16), pltpu.VMEM((n_chunks, 256, KEY_CHUNK), BF16),
            pltpu.VMEM((2 * DEC_SEQ, BRANCH_W), BF16),
            pltpu.VMEM((3, 2, n_chunks, KEY_CHUNK, tq), F32),
            pltpu.VMEM((2 * BRANCH_W, tq), F32),
        ],
        compiler_params=_params("arbitrary", "arbitrary"),
        name=f"lat_mix_{layer}",
    )(*pre, *caches, p["gdiff"], p["lamv"],
      p["dft_lat"], p["bdc"], p["bds"], p["w_fourier"], p["wsp"], p["bsp"])


def _merge_call(layer, x_ctx, x_lat, lat_row0, br_ctx, br_lat, mods, p, w, casts):
    tm = ROW_TM
    ctx_tiles = CTX_ROWS // tm
    lat_tile0 = lat_row0 // tm
    ctx_tile = lambda i: (jnp.minimum(i, ctx_tiles - 1), 0)
    jobs = [_cast_job(c, layer + 1) for c in casts]
    outs = pl.pallas_call(
        functools.partial(_merge_kernel, len(jobs)),
        grid=(ALL_ROWS // tm,),
        in_specs=[
            pl.BlockSpec((tm, D_MODEL), ctx_tile),
            pl.BlockSpec((tm, D_MODEL), lambda i: (lat_tile0 + jnp.maximum(i - ctx_tiles, 0), 0)),
            pl.BlockSpec((tm, D_MODEL), ctx_tile),
            pl.BlockSpec((tm, D_MODEL), lambda i: (jnp.maximum(i - ctx_tiles, 0), 0)),
            _mod_spec(_row_mod_select(tm)),
            _layer_block((1, D_MODEL), layer),
            _resident((D_MODEL, N_BRANCH * D_MODEL)),
            _resident((N_BRANCH * BRANCH_W, D_MODEL)),
            _resident((D_MODEL, D_MODEL)),
        ] + [j[0] for j in jobs],
        out_specs=[pl.BlockSpec((tm, D_MODEL), lambda i: (i, 0))] + [j[1] for j in jobs],
        out_shape=[jax.ShapeDtypeStruct((ALL_ROWS, D_MODEL), F32)] + [j[2] for j in jobs],
        compiler_params=_params("arbitrary"),
        name=f"merge_{layer}",
    )(x_ctx, x_lat, br_ctx, br_lat, mods, p["norm1_g"], w["w_gate"], w["w_branch"], w["w_out"], *casts)
    return outs[0], outs[1:]


def _mlp_call(layer, x, mods, p, w, casts=(), ada=None, final_g=None):
    tm = ROW_TM
    steps = ALL_ROWS // tm
    ctx_tiles = CTX_ROWS // tm
    in_specs = [
        pl.BlockSpec((tm, D_MODEL), lambda i: (i, 0)),
        _mod_spec(_row_mod_select(tm)),
        _layer_block((1, D_MODEL), layer),
        _resident((D_MODEL, D_FF)),
        _resident((D_FF, D_MODEL)),
    ]
    args = [x, mods, p["norm2_g"], w["w_mlp1"], w["w_mlp2"]]
    if final_g is None:
        ada_w = N_MOD * D_MODEL // steps
        jobs = [_cast_job(c, layer + 1) for c in casts]
        body = functools.partial(_mlp_kernel, len(jobs))
        in_specs += [
            _full((COND_ROWS, D_MODEL)),
            pl.BlockSpec((None, D_MODEL, ada_w), lambda i: (layer + 1, 0, i)),
            pl.BlockSpec((None, 1, ada_w), lambda i: (layer + 1, 0, i)),
        ] + [j[0] for j in jobs]
        args += list(ada) + list(casts)
        out_specs = [pl.BlockSpec((tm, D_MODEL), lambda i: (i, 0)),
                     pl.BlockSpec((COND_ROWS, ada_w), lambda i: (0, i))] + [j[1] for j in jobs]
        out_shape = [jax.ShapeDtypeStruct((ALL_ROWS, D_MODEL), F32),
                     jax.ShapeDtypeStruct((COND_ROWS, N_MOD * D_MODEL), F32)] + [j[2] for j in jobs]
    else:
        body = _mlp_final_kernel
        in_specs.append(_full((1, D_MODEL)))
        args.append(final_g)
        out_specs = [
            pl.BlockSpec((tm, D_MODEL), lambda i: (jnp.minimum(i, ctx_tiles - 1), 0)),
            pl.BlockSpec((tm, D_MODEL), lambda i: (jnp.maximum(i - ctx_tiles, 0), 0)),
        ]
        out_shape = [jax.ShapeDtypeStruct((CTX_ROWS, D_MODEL), F32),
                     jax.ShapeDtypeStruct((LAT_ROWS, D_MODEL), F32)]
    return pl.pallas_call(
        body,
        grid=(steps,),
        in_specs=in_specs,
        out_specs=out_specs,
        out_shape=out_shape,
        compiler_params=_params("arbitrary"),
        name=f"mlp_{layer}",
    )(*args)


def kernel(x_prompt, x_sample, c, cache_gqa_k, cache_gqa_v, cache_diff_k, cache_diff_v, c_ctx, w_ada, b_ada, norm1_g, norm2_g, w_in, w_fourier, q_norm_g, k_norm_g, lambda_q1, lambda_k1, lambda_q2, lambda_k2, diff_norm_g, sgu_norm_g, w_spatial, b_spatial, w_gate, w_branch, w_out, w_mlp1, w_mlp2, final_norm_g):
    bdc, bds = _channel_dft_tables()
    cb, sb = _rope_tables(DEC_SEQ, GQA_HEAD_DIM)
    cc, sc = _rope_tables(DEC_SEQ, DIFF_QK_DIM)
    p = {
        "norm1_g": norm1_g.reshape(DEPTH, 1, D_MODEL),
        "norm2_g": norm2_g.reshape(DEPTH, 1, D_MODEL),
        "w_fourier": w_fourier.astype(BF16),
        "gq": jnp.tile(q_norm_g, (1, GQA_HEADS)).reshape(DEPTH, 1, 256),
        "gk": jnp.tile(k_norm_g, (1, GQA_KV_HEADS)).reshape(DEPTH, 1, 128),
        "gsgu": sgu_norm_g.reshape(DEPTH, 1, BRANCH_W),
        "gdiff": jnp.tile(diff_norm_g, (1, DIFF_HEADS)).reshape(DEPTH, 1, 256),
        "lamv": jnp.stack([lambda_q1, lambda_k1, lambda_q2, lambda_k2], axis=1),
        "wsp": w_spatial.reshape(DEPTH, SGU_GROUPS * SGU_CHUNK, SGU_CHUNK).astype(BF16),
        "bsp": jnp.repeat(jnp.swapaxes(b_spatial, 1, 2), BRANCH_W // SGU_GROUPS, axis=2),
        "dft_ctx": jnp.asarray(_dft_tables(SEQ)).astype(BF16),
        "dft_lat": jnp.asarray(_dft_tables(DEC_SEQ)).astype(BF16),
        "bdc": jnp.asarray(bdc).astype(BF16),
        "bds": jnp.asarray(bds).astype(BF16),
        "rope_cb": jnp.asarray(cb), "rope_sb": jnp.asarray(sb),
        "rope_cc": jnp.asarray(cc), "rope_sc": jnp.asarray(sc),
    }
    lat_caches = (
        cache_gqa_k.reshape(DEC_BATCH, DEPTH, PAST_LEN, 128),
        cache_gqa_v.reshape(DEC_BATCH, DEPTH, PAST_LEN, 128),
        cache_diff_k.reshape(DEC_BATCH, DEPTH, PAST_LEN, 256),
        cache_diff_v.reshape(DEC_BATCH, DEPTH, PAST_LEN, 256),
    )

    cond = jnp.concatenate(
        [c_ctx[None, :], c, jnp.zeros((COND_ROWS - 1 - DEC_BATCH, D_MODEL), F32)], axis=0)
    ada = (cond, w_ada, b_ada.reshape(DEPTH, 1, N_MOD * D_MODEL))
    mods = _ada_call(*ada).reshape(COND_ROWS, N_MOD, D_MODEL)

    merge_casts = {"w_in": w_in, "w_gate": w_gate}
    mlp_casts = {"w_branch": w_branch.reshape(DEPTH, N_BRANCH * BRANCH_W, D_MODEL), "w_out": w_out,
                 "w_mlp1": w_mlp1, "w_mlp2": w_mlp2}
    w = {k: v[0].astype(BF16) for k, v in {**merge_casts, **mlp_casts}.items()}

    x_ctx, x_lat, lat_row0 = x_prompt.reshape(CTX_ROWS, D_MODEL), x_sample.reshape(LAT_ROWS, D_MODEL), 0
    new_caches = None
    for layer in range(DEPTH):
        last = layer + 1 == DEPTH
        br_ctx, new_caches = _ctx_call(layer, x_ctx, mods, p, w, new_caches)
        pre = _lat_pre_call(layer, x_lat, lat_row0, mods, p, w)
        br_lat = _lat_mix_call(layer, pre, lat_caches, p)
        x, cast_a = _merge_call(layer, x_ctx, x_lat, lat_row0, br_ctx, br_lat, mods, p, w,
                                [] if last else list(merge_casts.values()))
        if last:
            y_ctx, y_lat = _mlp_call(layer, x, mods, p, w, final_g=final_norm_g.reshape(1, D_MODEL))
        else:
            outs = _mlp_call(layer, x, mods, p, w, list(mlp_casts.values()), ada)
            x, mods, cast_b = outs[0], outs[1].reshape(COND_ROWS, N_MOD, D_MODEL), outs[2:]
            w = dict(zip(list(merge_casts) + list(mlp_casts), list(cast_a) + list(cast_b)))
            x_ctx, x_lat, lat_row0 = x, x, CTX_ROWS

    gk, gv, dk, dv = new_caches
    return (
        y_ctx.reshape(BATCH, SEQ, D_MODEL),
        y_lat.reshape(DEC_BATCH, DEC_SEQ, D_MODEL),
        gk.reshape(BATCH, DEPTH, SEQ, GQA_KV_HEADS, GQA_HEAD_DIM),
        gv.reshape(BATCH, DEPTH, SEQ, GQA_KV_HEADS, GQA_HEAD_DIM),
        dk.reshape(BATCH, DEPTH, SEQ, DIFF_HEADS, 2, DIFF_QK_DIM),
        dv.reshape(BATCH, DEPTH, SEQ, DIFF_HEADS, DIFF_V_DIM),
    )
```

```python
import functools
import math

import numpy as np
import jax
import jax.numpy as jnp
from jax import lax
from jax.experimental import pallas as pl
from jax.experimental.pallas import tpu as pltpu

F32 = jnp.float32
BF16 = jnp.bfloat16

D_MODEL = 1024
BATCH = 32
SEQ = 256
DEPTH = 4
DEC_BATCH = 2
DEC_SEQ = 2048
PAST_LEN = 512
GRID_W = 64
ROPE_THETA = 10000.0
EPS = 1e-6
N_BRANCH = 4
BRANCH_W = 256
FNET_GW = 64
GQA_HEADS = 4
GQA_KV_HEADS = 2
GQA_HEAD_DIM = 64
DIFF_HEADS = 4
DIFF_V_DIM = 64
DIFF_QK_DIM = 32
SGU_GROUPS = 4
SGU_CHUNK = 128
D_FF = 4 * D_MODEL
N_MOD = 6
IN_W = 2048

CTX_ROWS = BATCH * SEQ
LAT_ROWS = DEC_BATCH * DEC_SEQ
ALL_ROWS = CTX_ROWS + LAT_ROWS
KEYS_LAT = PAST_LEN + DEC_SEQ

LANES = 128
VMEM_LIMIT = 52 * 1024 * 1024

CTX_SEQS_PER_STEP = 4
LAT_PRE_TM = 512
LAT_TQ = 256
KEY_CHUNK = 256
ROW_TM = 512
COND_ROWS = 8
CAST_STEPS = 16

_A0, _QB0, _KB0, _VB0, _QC0, _KC0, _VC0, _DU0, _DV0 = 0, 256, 512, 640, 768, 1024, 1280, 1536, 1792
_Q_HEAD_ORDER = (0, 2, 1, 3)
_LOG2E = math.log2(math.e)
_GQA_Q_SCALE = _LOG2E / math.sqrt(GQA_HEAD_DIM)
_DIFF_Q_SCALE = _LOG2E / math.sqrt(DIFF_QK_DIM)


def _dft_tables(length):
    k = np.arange(length, dtype=np.int64)
    idx = (k[:, None] * k[None, :]) % length
    ang = 2.0 * np.pi * idx.astype(np.float64) / length
    return np.concatenate([np.cos(ang), -np.sin(ang)], axis=1).astype(np.float32)


def _channel_dft_tables():
    k = np.arange(FNET_GW, dtype=np.int64)
    idx = (k[:, None] * k[None, :]) % FNET_GW
    ang = 2.0 * np.pi * idx.astype(np.float64) / FNET_GW
    c = np.cos(ang) / math.sqrt(FNET_GW)
    s = np.sin(ang) / math.sqrt(FNET_GW)
    eye = np.eye(BRANCH_W // FNET_GW)
    return (np.kron(eye, c).astype(np.float32), np.kron(eye, s).astype(np.float32))


def _rope_tables(length, dim):
    rows = length // GRID_W
    row = np.repeat(np.arange(rows, dtype=np.float64), GRID_W)
    col = np.tile(np.arange(GRID_W, dtype=np.float64), rows)
    quarter = dim // 4
    inv = ROPE_THETA ** (-np.arange(quarter, dtype=np.float64) / quarter)
    ang = np.concatenate([row[:, None] * inv, col[:, None] * inv], axis=-1)
    c = np.concatenate([np.cos(ang), np.cos(ang)], axis=-1)
    s = np.concatenate([-np.sin(ang), np.sin(ang)], axis=-1)
    reps = LANES // dim
    return (np.tile(c, (1, reps)).astype(np.float32), np.tile(s, (1, reps)).astype(np.float32))


def _dot(a, b):
    return jnp.dot(a, b, preferred_element_type=F32)


def _rms_mod(x, g, scale, shift, out_scale=1.0):
    y = x * lax.rsqrt(jnp.mean(x * x, axis=-1, keepdims=True) + EPS)
    return (y * (g * (1.0 + scale) * out_scale) + shift * out_scale).astype(BF16)


def _gelu(x):
    c = math.sqrt(2.0 / math.pi)
    return 0.5 * x * (1.0 + jnp.tanh(c * (x + 0.044715 * (x * x * x))))


def _group_ones(width, group):
    r = lax.broadcasted_iota(jnp.int32, (width, width), 0) // group
    c = lax.broadcasted_iota(jnp.int32, (width, width), 1) // group
    return (r == c).astype(BF16)


def _group_mean_sq(x, group):
    x2 = x * x
    hi = x2.astype(BF16)
    lo = (x2 - hi.astype(F32)).astype(BF16)
    ones = _group_ones(x.shape[-1], group)
    return (_dot(hi, ones) + _dot(lo, ones)) * (1.0 / group)


def _lane_mask(lo, hi):
    lane = lax.broadcasted_iota(jnp.int32, (1, LANES), 1)
    return (lane >= lo) & (lane < hi)


def _swap_halves(x, dim):
    n = x.shape[-1]
    half = dim // 2
    lane = lax.broadcasted_iota(jnp.int32, (1, n), 1)
    fwd = pltpu.roll(x, n - half, 1)
    bwd = pltpu.roll(x, half, 1)
    return jnp.where((lane % dim) < half, fwd, bwd)


def _rope(x, cos, sin, dim):
    reps = x.shape[-1] // LANES
    if reps > 1:
        cos = jnp.concatenate([cos] * reps, axis=1)
        sin = jnp.concatenate([sin] * reps, axis=1)
    return x * cos + _swap_halves(x, dim) * sin


def _pair_q_heads(q):
    b0, b1 = q[:, :LANES], q[:, LANES:]
    lo = _lane_mask(0, GQA_HEAD_DIM)
    r0 = pltpu.roll(b0, GQA_HEAD_DIM, 1)
    r1 = pltpu.roll(b1, GQA_HEAD_DIM, 1)
    return jnp.concatenate([jnp.where(lo, b0, r1), jnp.where(lo, r0, b1)], axis=1)


def _fold8(x, op):
    return op(x.reshape(x.shape[0] // 8, 8, x.shape[1]), axis=0)


class _Unit:
    def __init__(self, k_ref, k_lanes, qt_blk, masks, vt_ref, v_rows, out_row0, is_diff):
        self.k_ref, self.k_lanes, self.vt_ref, self.v_rows = k_ref, k_lanes, vt_ref, v_rows
        self.out_row0, self.is_diff = out_row0, is_diff
        row = lax.broadcasted_iota(jnp.int32, (LANES, 1), 0)
        self.qm = [jnp.where((row >= lo) & (row < hi), qt_blk, jnp.zeros_like(qt_blk))
                   for lo, hi in masks]
        self.m = self.scale = None


def _make_units(qbt, qct, kb_ref, kc_ref, vbt_ref, vct_ref):
    gqa_units, diff_units = [], []
    for pos, head in enumerate(_Q_HEAD_ORDER):
        blk, kv = pos // 2, pos % 2
        rows = slice(kv * GQA_HEAD_DIM, (kv + 1) * GQA_HEAD_DIM)
        gqa_units.append(_Unit(kb_ref, slice(0, LANES), qbt[blk * LANES:(blk + 1) * LANES, :],
                               [(rows.start, rows.stop)], vbt_ref, rows, head * GQA_HEAD_DIM, False))
    for head in range(DIFF_HEADS):
        blk, base = head // 2, (head % 2) * DIFF_V_DIM
        lanes = slice(blk * LANES, (blk + 1) * LANES)
        diff_units.append(_Unit(kc_ref, lanes, qct[lanes, :],
                                [(base, base + DIFF_QK_DIM), (base + DIFF_QK_DIM, base + DIFF_V_DIM)],
                                vct_ref, slice(head * DIFF_V_DIM, (head + 1) * DIFF_V_DIM),
                                BRANCH_W + head * DIFF_V_DIM, True))
    return [u for pair in zip(gqa_units, diff_units) for u in pair]


def _attention_pipeline(units, lam, s_buf, ot_ref):
    n = len(units)
    tq = s_buf.shape[-1]
    for j in range(n + 2):
        ua = units[j] if j < n else None
        ub = units[j - 1] if 1 <= j <= n else None
        uc = units[j - 2] if 2 <= j <= n + 1 and units[j - 2].is_diff else None
        sa, sb, sc = j % 3, (j - 1) % 3, (j - 2) % 3
        init = {}
        if ua is not None:
            init["mx"] = [jnp.full((8, tq), -jnp.inf, F32) for _ in ua.qm]
        if ub is not None:
            init["l"] = [jnp.zeros((8, tq), F32) for _ in ub.qm]
            if not ub.is_diff:
                init["ob"] = jnp.zeros((DIFF_V_DIM, tq), F32)
        if uc is not None:
            init["oc"] = jnp.zeros((DIFF_V_DIM, tq), F32)

        def body(c, carry, ua=ua, ub=ub, uc=uc, sa=sa, sb=sb, sc=sc):
            new = dict(carry)
            if ua is not None:
                k_c = ua.k_ref[c, :, ua.k_lanes]
                mx = []
                for mi, qm in enumerate(ua.qm):
                    s = _dot(k_c, qm)
                    s_buf[sa, mi, c] = s
                    mx.append(jnp.maximum(carry["mx"][mi], _fold8(s, jnp.max)))
                new["mx"] = mx
            if ub is not None:
                ls = []
                for mi in range(len(ub.qm)):
                    e = jnp.exp2(s_buf[sb, mi, c] - ub.m[mi])
                    ls.append(carry["l"][mi] + _fold8(e, jnp.sum))
                    if ub.is_diff:
                        s_buf[sb, mi, c] = e
                    else:
                        new["ob"] = carry["ob"] + _dot(ub.vt_ref[c, ub.v_rows, :], e.astype(BF16))
                new["l"] = ls
            if uc is not None:
                w = s_buf[sc, 0, c] - s_buf[sc, 1, c] * uc.scale[1]
                new["oc"] = carry["oc"] + _dot(uc.vt_ref[c, uc.v_rows, :], w.astype(BF16))
            return new

        out = lax.fori_loop(0, s_buf.shape[2], body, init, unroll=True)
        if ua is not None:
            ua.m = [jnp.max(mx, axis=0, keepdims=True) for mx in out["mx"]]
        if ub is not None:
            l = [jnp.sum(x, axis=0, keepdims=True) for x in out["l"]]
            if ub.is_diff:
                ub.scale = [1.0 / l[0], lam * l[0] / l[1]]
            else:
                ot_ref[ub.out_row0:ub.out_row0 + DIFF_V_DIM, :] = out["ob"] * (1.0 / l[0])
        if uc is not None:
            o = out["oc"] * uc.scale[0]
            ot_ref[uc.out_row0:uc.out_row0 + DIFF_V_DIM, :] = (
                o * lax.rsqrt(jnp.mean(o * o, axis=0, keepdims=True) + EPS))


def _lambda(lamv, layer):
    lam_init = 0.8 - 0.6 * math.exp(-0.3 * layer)
    a = jnp.sum(lamv[0:1] * lamv[1:2], axis=-1, keepdims=True)
    b = jnp.sum(lamv[2:3] * lamv[3:4], axis=-1, keepdims=True)
    return jnp.exp(a) - jnp.exp(b) + lam_init, 1.0 - lam_init


def _sgu(du, dvn, wsp, bias):
    lane = lax.broadcasted_iota(jnp.int32, (1, BRANCH_W), 1)
    outs = []
    for n in range(du.shape[0] // SGU_CHUNK):
        rows = slice(n * SGU_CHUNK, (n + 1) * SGU_CHUNK)
        r = _dot(wsp, dvn[rows].astype(BF16))
        s = bias
        for g in range(SGU_GROUPS):
            gm = (lane >= g * 64) & (lane < (g + 1) * 64)
            s = s + jnp.where(gm, r[g * SGU_CHUNK:(g + 1) * SGU_CHUNK], 0.0)
        outs.append(du[rows] * s)
    return outs[0] if len(outs) == 1 else jnp.concatenate(outs, axis=0)


def _fourier_stage1(a, bdc, bds):
    return jnp.concatenate([_dot(a, bdc), _dot(a, bds)], axis=0).astype(BF16)


def _fourier_stage2(dft_rows, f, length, w_fourier):
    y = _dot(dft_rows, f) * (1.0 / math.sqrt(length))
    return _dot(y.astype(BF16), w_fourier)


def _ada_kernel(cond_ref, w_ref, b_ref, o_ref):
    c = cond_ref[...]
    s = c * jax.nn.sigmoid(c)
    o_ref[...] = _dot(s.astype(BF16), w_ref[...].astype(BF16)) + b_ref[...]


def _ctx_kernel(layer, x_ref, mod_ref, g1_ref, win_ref, gq_ref, gk_ref, gsgu_ref, gdiff_ref,
                lamv_ref, dft_ref, bdc_ref, bds_ref, wf_ref, wsp_ref, bsp_ref,
                br_ref, ok_ref, ov_ref, odk_ref, odv_ref,
                kb_s, vbt_s, kc_s, vct_s, s_buf, ot_s):
    x = x_ref[...]
    mod = mod_ref[...]
    h = _rms_mod(x, g1_ref[...], mod[1:2], mod[0:1])
    proj = _dot(h, win_ref[...])
    qb = proj[:, _QB0:_KB0]
    kb = proj[:, _KB0:_VB0]
    qb = _pair_q_heads(qb * lax.rsqrt(_group_mean_sq(qb, GQA_HEAD_DIM) + EPS) * gq_ref[...])
    kb = kb * lax.rsqrt(_group_mean_sq(kb, GQA_HEAD_DIM) + EPS) * gk_ref[...]
    vb = proj[:, _VB0:_QC0]
    qc = proj[:, _QC0:_KC0]
    kc = proj[:, _KC0:_VC0]
    vc = proj[:, _VC0:_DU0]
    du = _gelu(proj[:, _DU0:_DV0])
    dv = _gelu(proj[:, _DV0:IN_W])
    dvn = dv * lax.rsqrt(jnp.mean(dv * dv, axis=-1, keepdims=True) + EPS) * gsgu_ref[...]
    lam, lam_scale = _lambda(lamv_ref[...], layer)
    qb = qb * _GQA_Q_SCALE
    qc = qc * _DIFF_Q_SCALE
    a = proj[:, _A0:_QB0].astype(BF16)
    for s in range(CTX_SEQS_PER_STEP):
        rows = slice(s * SEQ, (s + 1) * SEQ)
        ok_ref[s] = kb[rows]
        ov_ref[s] = vb[rows]
        odk_ref[s] = kc[rows]
        odv_ref[s] = vc[rows]
        kb_s[s, 0] = kb[rows].astype(BF16)
        kc_s[s, 0] = kc[rows].astype(BF16)
        vbt_s[s, 0] = vb[rows].T.astype(BF16)
        vct_s[s, 0] = vc[rows].T.astype(BF16)
        f = _fourier_stage1(a[rows], bdc_ref[...], bds_ref[...])
        br_ref[rows, 0:256] = _fourier_stage2(dft_ref[...], f, SEQ, wf_ref[...]).astype(BF16)
        br_ref[rows, 768:1024] = _sgu(du[rows], dvn[rows], wsp_ref[...], bsp_ref[...]).astype(BF16)
    for s in range(CTX_SEQS_PER_STEP):
        rows = slice(s * SEQ, (s + 1) * SEQ)
        units = _make_units(qb[rows].T.astype(BF16), qc[rows].T.astype(BF16),
                            kb_s.at[s], kc_s.at[s], vbt_s.at[s], vct_s.at[s])
        _attention_pipeline(units, lam, s_buf.at[s], ot_s.at[s])
        br_ref[rows, 256:512] = ot_s[s, 0:BRANCH_W, :].T.astype(BF16)
        br_ref[rows, 512:768] = (ot_s[s, BRANCH_W:2 * BRANCH_W, :].T
                                 * (gdiff_ref[...] * lam_scale)).astype(BF16)


def _lat_pre_kernel(x_ref, mod_ref, g1_ref, win_ref, gq_ref, gk_ref, gsgu_ref,
                    cb_ref, sb_ref, cc_ref, sc_ref,
                    a_ref, qb_ref, kb_ref, vbt_ref, qc_ref, kc_ref, vct_ref, du_ref, dvn_ref):
    x = x_ref[...]
    mod = mod_ref[...]
    h = _rms_mod(x, g1_ref[...], mod[1:2], mod[0:1])
    cb, sb, cc, sc = cb_ref[...], sb_ref[...], cc_ref[...], sc_ref[...]
    proj = lambda c0, c1: _dot(h, win_ref[:, c0:c1])
    qb = proj(_QB0, _KB0)
    kvb = proj(_KB0, _QC0)
    kb, vb = kvb[:, :LANES], kvb[:, LANES:]
    qb = qb * lax.rsqrt(_group_mean_sq(qb, GQA_HEAD_DIM) + EPS) * gq_ref[...]
    kb = kb * lax.rsqrt(_group_mean_sq(kb, GQA_HEAD_DIM) + EPS) * gk_ref[...]
    qb_ref[...] = _pair_q_heads(_rope(qb, cb, sb, GQA_HEAD_DIM) * _GQA_Q_SCALE).T.astype(BF16)
    kb_ref[...] = _rope(kb, cb, sb, GQA_HEAD_DIM).astype(BF16)
    vbt_ref[...] = vb.T.astype(BF16)
    qc = proj(_QC0, _KC0)
    kc = proj(_KC0, _VC0)
    qc_ref[...] = (_rope(qc, cc, sc, DIFF_QK_DIM) * _DIFF_Q_SCALE).T.astype(BF16)
    kc_ref[...] = _rope(kc, cc, sc, DIFF_QK_DIM).astype(BF16)
    vc = proj(_VC0, _DU0)
    du = proj(_DU0, _DV0)
    vct_ref[...] = vc.T.astype(BF16)
    du_ref[...] = _gelu(du)
    dv = _gelu(proj(_DV0, IN_W))
    a_ref[...] = proj(_A0, _QB0).astype(BF16)
    dvn_ref[...] = (dv * lax.rsqrt(jnp.mean(dv * dv, axis=-1, keepdims=True) + EPS)
                    * gsgu_ref[...]).astype(BF16)


def _lat_mix_kernel(layer, a_ref, qb_ref, kb_ref, vbt_ref, qc_ref, kc_ref, vct_ref,
                    du_ref, dvn_ref, ck_ref, cv_ref, cdk_ref, cdv_ref, gdiff_ref, lamv_ref,
                    dft_ref, bdc_ref, bds_ref, wf_ref, wsp_ref, bsp_ref,
                    br_ref, kb_s, vbt_s, kc_s, vct_s, f_s, s_buf, ot_s):
    kc = KEY_CHUNK
    past_chunks = PAST_LEN // kc

    @pl.when(pl.program_id(1) == 0)
    def _fill():
        for c in range(KEYS_LAT // kc):
            if c < past_chunks:
                rows = slice(c * kc, (c + 1) * kc)
                kb_s[c] = ck_ref[rows, :].astype(BF16)
                kc_s[c] = cdk_ref[rows, :].astype(BF16)
                vbt_s[c] = cv_ref[rows, :].T.astype(BF16)
                vct_s[c] = cdv_ref[rows, :].T.astype(BF16)
            else:
                rows = slice((c - past_chunks) * kc, (c - past_chunks + 1) * kc)
                kb_s[c] = kb_ref[rows, :]
                kc_s[c] = kc_ref[rows, :]
                vbt_s[c] = vbt_ref[:, rows]
                vct_s[c] = vct_ref[:, rows]
        f_s[...] = _fourier_stage1(a_ref[...], bdc_ref[...], bds_ref[...])

    lam, lam_scale = _lambda(lamv_ref[...], layer)
    units = _make_units(qb_ref[...], qc_ref[...], kb_s, kc_s, vbt_s, vct_s)
    _attention_pipeline(units, lam, s_buf, ot_s)
    br_ref[:, 0:256] = _fourier_stage2(dft_ref[...], f_s[...], DEC_SEQ, wf_ref[...]).astype(BF16)
    br_ref[:, 768:1024] = _sgu(du_ref[...], dvn_ref[...].astype(F32), wsp_ref[...],
                               bsp_ref[...]).astype(BF16)
    br_ref[:, 256:512] = ot_s[0:BRANCH_W, :].T.astype(BF16)
    br_ref[:, 512:768] = (ot_s[BRANCH_W:2 * BRANCH_W, :].T * (gdiff_ref[...] * lam_scale)).astype(BF16)


def _cast_slabs(n_cast, refs):
    @pl.when(pl.program_id(0) < CAST_STEPS)
    def _():
        for src, dst in zip(refs[:n_cast], refs[n_cast:]):
            dst[...] = src[...].astype(BF16)


def _merge_kernel(n_cast, xc_ref, xl_ref, brc_ref, brl_ref, mod_ref, g1_ref, wg_ref, wb_ref, wo_ref,
                  *rest):
    cast_src, o_ref, cast_dst = rest[:n_cast], rest[n_cast], rest[n_cast + 1:]
    is_ctx = pl.program_id(0) < CTX_ROWS // ROW_TM
    x = jnp.where(is_ctx, xc_ref[...], xl_ref[...])
    br = jnp.where(is_ctx, brc_ref[...], brl_ref[...])
    mod = mod_ref[...]
    h_half = _rms_mod(x, g1_ref[...], mod[1:2], mod[0:1], out_scale=0.5)
    br_half = br * 0.5
    merged = None
    for n in range(N_BRANCH):
        t = jnp.tanh(_dot(h_half, wg_ref[:, n * D_MODEL:(n + 1) * D_MODEL]))
        term = (t + 1.0) * _dot(br_half[:, n * BRANCH_W:(n + 1) * BRANCH_W],
                                wb_ref[n * BRANCH_W:(n + 1) * BRANCH_W, :])
        merged = term if merged is None else merged + term
    o_ref[...] = x + mod[2:3] * _dot(merged.astype(BF16), wo_ref[...])
    _cast_slabs(n_cast, cast_src + cast_dst)


def _mlp_block(x, mod, g2, w1_ref, w2_ref):
    h = _rms_mod(x, g2, mod[4:5], mod[3:4])
    acc = None
    for n in range(D_FF // D_MODEL):
        cols = slice(n * D_MODEL, (n + 1) * D_MODEL)
        hid = jnp.maximum(_dot(h, w1_ref[:, cols]), 0.0)
        term = _dot((hid * hid).astype(BF16), w2_ref[cols, :])
        acc = term if acc is None else acc + term
    return x + mod[5:6] * acc


def _mlp_kernel(n_cast, x_ref, mod_ref, g2_ref, w1_ref, w2_ref, cond_ref, wada_ref, bada_ref, *rest):
    cast_src, o_ref, modn_ref, cast_dst = rest[:n_cast], rest[n_cast], rest[n_cast + 1], rest[n_cast + 2:]
    o_ref[...] = _mlp_block(x_ref[...], mod_ref[...], g2_ref[...], w1_ref, w2_ref)
    _ada_kernel(cond_ref, wada_ref, bada_ref, modn_ref)
    _cast_slabs(n_cast, cast_src + cast_dst)


def _mlp_final_kernel(x_ref, mod_ref, g2_ref, w1_ref, w2_ref, gf_ref, yc_ref, yl_ref):
    x = _mlp_block(x_ref[...], mod_ref[...], g2_ref[...], w1_ref, w2_ref)
    y = x * lax.rsqrt(jnp.mean(x * x, axis=-1, keepdims=True) + EPS) * gf_ref[...]
    is_ctx = pl.program_id(0) < CTX_ROWS // ROW_TM

    @pl.when(is_ctx)
    def _():
        yc_ref[...] = y

    @pl.when(jnp.logical_not(is_ctx))
    def _():
        yl_ref[...] = y


def _params(*sem):
    return pltpu.CompilerParams(dimension_semantics=sem, vmem_limit_bytes=VMEM_LIMIT)


def _full(shape):
    nd = len(shape)
    return pl.BlockSpec(shape, lambda *_: (0,) * nd)


def _layer_block(shape, layer):
    nd = len(shape)
    return pl.BlockSpec((None,) + tuple(shape), lambda *_: (layer,) + (0,) * nd)


def _resident(shape):
    nd = len(shape)
    return pl.BlockSpec(shape, lambda *_: (0,) * nd, pipeline_mode=pl.Buffered(1))


def _cast_job(w, layer):
    _, r, c = w.shape
    slab = r // CAST_STEPS
    step = lambda i: jnp.minimum(i, CAST_STEPS - 1)
    return (pl.BlockSpec((None, slab, c), lambda i: (layer, step(i), 0)),
            pl.BlockSpec((slab, c), lambda i: (step(i), 0)),
            jax.ShapeDtypeStruct((r, c), BF16))


def _mod_spec(select):
    return pl.BlockSpec((None, N_MOD, D_MODEL), lambda *idx: (select(*idx), 0, 0))


def _row_mod_select(tm):
    ctx_tiles = CTX_ROWS // tm
    tiles_per_seq = DEC_SEQ // tm
    return lambda i: jnp.where(i < ctx_tiles, 0, 1 + (i - ctx_tiles) // tiles_per_seq)


def _ada_call(cond, w_ada, b_ada):
    return pl.pallas_call(
        _ada_kernel,
        grid=(N_MOD,),
        in_specs=[
            pl.BlockSpec((COND_ROWS, D_MODEL), lambda j: (0, 0)),
            pl.BlockSpec((None, D_MODEL, D_MODEL), lambda j: (0, 0, j)),
            pl.BlockSpec((None, 1, D_MODEL), lambda j: (0, 0, j)),
        ],
        out_specs=pl.BlockSpec((COND_ROWS, D_MODEL), lambda j: (0, j)),
        out_shape=jax.ShapeDtypeStruct((COND_ROWS, N_MOD * D_MODEL), F32),
        compiler_params=_params("arbitrary"),
        name="ada",
    )(cond, w_ada, b_ada)


def _ctx_call(layer, x, mods, p, w, caches):
    ns = CTX_SEQS_PER_STEP
    tm = ns * SEQ
    cache_shapes = [(BATCH, DEPTH, SEQ, 128), (BATCH, DEPTH, SEQ, 128),
                    (BATCH, DEPTH, SEQ, 256), (BATCH, DEPTH, SEQ, 256)]
    in_specs = [
        pl.BlockSpec((tm, D_MODEL), lambda i: (i, 0)),
        _mod_spec(lambda i: 0),
        _layer_block((1, D_MODEL), layer),
        _resident((D_MODEL, IN_W)),
        _layer_block((1, 256), layer),
        _layer_block((1, 128), layer),
        _layer_block((1, 256), layer),
        _layer_block((1, 256), layer),
        _layer_block((4, DIFF_QK_DIM), layer),
        _full((SEQ, 2 * SEQ)),
        _full((BRANCH_W, BRANCH_W)),
        _full((BRANCH_W, BRANCH_W)),
        _layer_block((BRANCH_W, BRANCH_W), layer),
        _layer_block((SGU_GROUPS * SGU_CHUNK, SGU_CHUNK), layer),
        _layer_block((SGU_CHUNK, BRANCH_W), layer),
    ]
    args = [x, mods, p["norm1_g"], w["w_in"], p["gq"], p["gk"], p["gsgu"], p["gdiff"], p["lamv"],
            p["dft_ctx"], p["bdc"], p["bds"], p["w_fourier"], p["wsp"], p["bsp"]]
    aliases = {}
    if caches is not None:
        for j, cache in enumerate(caches):
            in_specs.append(pl.BlockSpec(memory_space=pl.ANY))
            aliases[len(args)] = 1 + j
            args.append(cache)
    kernel = functools.partial(_ctx_kernel, layer)
    if caches is not None:
        base = kernel

        def kernel(*refs):
            n_in = 15
            return base(*refs[:n_in], *refs[n_in + 4:])
    out_specs = [pl.BlockSpec((tm, D_MODEL), lambda i: (i, 0))] + [
        pl.BlockSpec((ns, None, SEQ, s[-1]), lambda i: (i, layer, 0, 0)) for s in cache_shapes]
    out_shape = [jax.ShapeDtypeStruct((CTX_ROWS, D_MODEL), BF16)] + [
        jax.ShapeDtypeStruct(s, F32) for s in cache_shapes]
    outs = pl.pallas_call(
        kernel,
        grid=(BATCH // ns,),
        in_specs=in_specs,
        out_specs=out_specs,
        out_shape=out_shape,
        scratch_shapes=[
            pltpu.VMEM((ns, 1, SEQ, 128), BF16), pltpu.VMEM((ns, 1, 128, SEQ), BF16),
            pltpu.VMEM((ns, 1, SEQ, 256), BF16), pltpu.VMEM((ns, 1, 256, SEQ), BF16),
            pltpu.VMEM((ns, 3, 2, 1, SEQ, SEQ), F32),
            pltpu.VMEM((ns, 2 * BRANCH_W, SEQ), F32),
        ],
        input_output_aliases=aliases,
        compiler_params=_params("arbitrary"),
        name=f"ctx_mix_{layer}",
    )(*args)
    return outs[0], tuple(outs[1:])


def _lat_pre_call(layer, x, row0, mods, p, w):
    tm = LAT_PRE_TM
    tile0 = row0 // tm
    tiles_per_seq = DEC_SEQ // tm
    rope_spec = pl.BlockSpec((tm, LANES), lambda i: (i % tiles_per_seq, 0))
    rows = lambda w, dt: (pl.BlockSpec((tm, w), lambda i: (i, 0)),
                          jax.ShapeDtypeStruct((LAT_ROWS, w), dt))
    cols = lambda w: (pl.BlockSpec((None, w, tm), lambda i: (i // tiles_per_seq, 0, i % tiles_per_seq)),
                      jax.ShapeDtypeStruct((DEC_BATCH, w, DEC_SEQ), BF16))
    outs = [rows(256, BF16), cols(256), rows(128, BF16), cols(128),
            cols(256), rows(256, BF16), cols(256), rows(256, F32), rows(256, BF16)]
    return pl.pallas_call(
        _lat_pre_kernel,
        grid=(LAT_ROWS // tm,),
        in_specs=[
            pl.BlockSpec((tm, D_MODEL), lambda i: (tile0 + i, 0)),
            _mod_spec(lambda i: 1 + i // tiles_per_seq),
            _layer_block((1, D_MODEL), layer),
            _resident((D_MODEL, IN_W)),
            _layer_block((1, 256), layer),
            _layer_block((1, 128), layer),
            _layer_block((1, 256), layer),
            rope_spec, rope_spec, rope_spec, rope_spec,
        ],
        out_specs=[spec for spec, _ in outs],
        out_shape=[shape for _, shape in outs],
        compiler_params=_params("arbitrary"),
        name=f"lat_pre_{layer}",
    )(x, mods, p["norm1_g"], w["w_in"], p["gq"], p["gk"], p["gsgu"],
      p["rope_cb"], p["rope_sb"], p["rope_cc"], p["rope_sc"])


def _lat_mix_call(layer, pre, caches, p):
    tq = LAT_TQ
    nq = DEC_SEQ // tq
    n_chunks = KEYS_LAT // KEY_CHUNK
    seq_block = lambda w: pl.BlockSpec((DEC_SEQ, w), lambda b, q: (b, 0))
    seq_block_t = lambda w: pl.BlockSpec((None, w, DEC_SEQ), lambda b, q: (b, 0, 0))
    q_block = lambda w: pl.BlockSpec((tq, w), lambda b, q: (b * nq + q, 0))
    q_block_t = lambda w: pl.BlockSpec((None, w, tq), lambda b, q: (b, 0, q))
    cache_block = lambda w: pl.BlockSpec((None, None, PAST_LEN, w), lambda b, q: (b, layer, 0, 0))
    return pl.pallas_call(
        functools.partial(_lat_mix_kernel, layer),
        grid=(DEC_BATCH, nq),
        in_specs=[
            seq_block(256), q_block_t(256), seq_block(128), seq_block_t(128),
            q_block_t(256), seq_block(256), seq_block_t(256),
            q_block(256), q_block(256),
            cache_block(128), cache_block(128), cache_block(256), cache_block(256),
            _layer_block((1, 256), layer),
            _layer_block((4, DIFF_QK_DIM), layer),
            pl.BlockSpec((tq, 2 * DEC_SEQ), lambda b, q: (q, 0)),
            _full((BRANCH_W, BRANCH_W)),
            _full((BRANCH_W, BRANCH_W)),
            _layer_block((BRANCH_W, BRANCH_W), layer),
            _layer_block((SGU_GROUPS * SGU_CHUNK, SGU_CHUNK), layer),
            _layer_block((SGU_CHUNK, BRANCH_W), layer),
        ],
        out_specs=q_block(D_MODEL),
        out_shape=jax.ShapeDtypeStruct((LAT_ROWS, D_MODEL), BF16),
        scratch_shapes=[
            pltpu.VMEM((n_chunks, KEY_CHUNK, 128), BF16), pltpu.VMEM((n_chunks, 128, KEY_CHUNK), BF16),
            pltpu.VMEM((n_chunks, KEY_CHUNK, 256), BF16), pltpu.VMEM((n_chunks, 256, KEY_CHUNK), BF16),
            pltpu.VMEM((2 * DEC_SEQ, BRANCH_W), BF16),
            pltpu.VMEM((3, 2, n_chunks, KEY_CHUNK, tq), F32),
            pltpu.VMEM((2 * BRANCH_W, tq), F32),
        ],
        compiler_params=_params("arbitrary", "arbitrary"),
        name=f"lat_mix_{layer}",
    )(*pre, *caches, p["gdiff"], p["lamv"],
      p["dft_lat"], p["bdc"], p["bds"], p["w_fourier"], p["wsp"], p["bsp"])


def _merge_call(layer, x_ctx, x_lat, lat_row0, br_ctx, br_lat, mods, p, w, casts):
    tm = ROW_TM
    ctx_tiles = CTX_ROWS // tm
    lat_tile0 = lat_row0 // tm
    ctx_tile = lambda i: (jnp.minimum(i, ctx_tiles - 1), 0)
    jobs = [_cast_job(c, layer + 1) for c in casts]
    outs = pl.pallas_call(
        functools.partial(_merge_kernel, len(jobs)),
        grid=(ALL_ROWS // tm,),
        in_specs=[
            pl.BlockSpec((tm, D_MODEL), ctx_tile),
            pl.BlockSpec((tm, D_MODEL), lambda i: (lat_tile0 + jnp.maximum(i - ctx_tiles, 0), 0)),
            pl.BlockSpec((tm, D_MODEL), ctx_tile),
            pl.BlockSpec((tm, D_MODEL), lambda i: (jnp.maximum(i - ctx_tiles, 0), 0)),
            _mod_spec(_row_mod_select(tm)),
            _layer_block((1, D_MODEL), layer),
            _resident((D_MODEL, N_BRANCH * D_MODEL)),
            _resident((N_BRANCH * BRANCH_W, D_MODEL)),
            _resident((D_MODEL, D_MODEL)),
        ] + [j[0] for j in jobs],
        out_specs=[pl.BlockSpec((tm, D_MODEL), lambda i: (i, 0))] + [j[1] for j in jobs],
        out_shape=[jax.ShapeDtypeStruct((ALL_ROWS, D_MODEL), F32)] + [j[2] for j in jobs],
        compiler_params=_params("arbitrary"),
        name=f"merge_{layer}",
    )(x_ctx, x_lat, br_ctx, br_lat, mods, p["norm1_g"], w["w_gate"], w["w_branch"], w["w_out"], *casts)
    return outs[0], outs[1:]


def _mlp_call(layer, x, mods, p, w, casts=(), ada=None, final_g=None):
    tm = ROW_TM
    steps = ALL_ROWS // tm
    ctx_tiles = CTX_ROWS // tm
    in_specs = [
        pl.BlockSpec((tm, D_MODEL), lambda i: (i, 0)),
        _mod_spec(_row_mod_select(tm)),
        _layer_block((1, D_MODEL), layer),
        _resident((D_MODEL, D_FF)),
        _resident((D_FF, D_MODEL)),
    ]
    args = [x, mods, p["norm2_g"], w["w_mlp1"], w["w_mlp2"]]
    if final_g is None:
        ada_w = N_MOD * D_MODEL // steps
        jobs = [_cast_job(c, layer + 1) for c in casts]
        body = functools.partial(_mlp_kernel, len(jobs))
        in_specs += [
            _full((COND_ROWS, D_MODEL)),
            pl.BlockSpec((None, D_MODEL, ada_w), lambda i: (layer + 1, 0, i)),
            pl.BlockSpec((None, 1, ada_w), lambda i: (layer + 1, 0, i)),
        ] + [j[0] for j in jobs]
        args += list(ada) + list(casts)
        out_specs = [pl.BlockSpec((tm, D_MODEL), lambda i: (i, 0)),
                     pl.BlockSpec((COND_ROWS, ada_w), lambda i: (0, i))] + [j[1] for j in jobs]
        out_shape = [jax.ShapeDtypeStruct((ALL_ROWS, D_MODEL), F32),
                     jax.ShapeDtypeStruct((COND_ROWS, N_MOD * D_MODEL), F32)] + [j[2] for j in jobs]
    else:
        body = _mlp_final_kernel
        in_specs.append(_full((1, D_MODEL)))
        args.append(final_g)
        out_specs = [
            pl.BlockSpec((tm, D_MODEL), lambda i: (jnp.minimum(i, ctx_tiles - 1), 0)),
            pl.BlockSpec((tm, D_MODEL), lambda i: (jnp.maximum(i - ctx_tiles, 0), 0)),
        ]
        out_shape = [jax.ShapeDtypeStruct((CTX_ROWS, D_MODEL), F32),
                     jax.ShapeDtypeStruct((LAT_ROWS, D_MODEL), F32)]
    return pl.pallas_call(
        body,
        grid=(steps,),
        in_specs=in_specs,
        out_specs=out_specs,
        out_shape=out_shape,
        compiler_params=_params("arbitrary"),
        name=f"mlp_{layer}",
    )(*args)


def kernel(x_prompt, x_sample, c, cache_gqa_k, cache_gqa_v, cache_diff_k, cache_diff_v, c_ctx, w_ada, b_ada, norm1_g, norm2_g, w_in, w_fourier, q_norm_g, k_norm_g, lambda_q1, lambda_k1, lambda_q2, lambda_k2, diff_norm_g, sgu_norm_g, w_spatial, b_spatial, w_gate, w_branch, w_out, w_mlp1, w_mlp2, final_norm_g):
    bdc, bds = _channel_dft_tables()
    cb, sb = _rope_tables(DEC_SEQ, GQA_HEAD_DIM)
    cc, sc = _rope_tables(DEC_SEQ, DIFF_QK_DIM)
    p = {
        "norm1_g": norm1_g.reshape(DEPTH, 1, D_MODEL),
        "norm2_g": norm2_g.reshape(DEPTH, 1, D_MODEL),
        "w_fourier": w_fourier.astype(BF16),
        "gq": jnp.tile(q_norm_g, (1, GQA_HEADS)).reshape(DEPTH, 1, 256),
        "gk": jnp.tile(k_norm_g, (1, GQA_KV_HEADS)).reshape(DEPTH, 1, 128),
        "gsgu": sgu_norm_g.reshape(DEPTH, 1, BRANCH_W),
        "gdiff": jnp.tile(diff_norm_g, (1, DIFF_HEADS)).reshape(DEPTH, 1, 256),
        "lamv": jnp.stack([lambda_q1, lambda_k1, lambda_q2, lambda_k2], axis=1),
        "wsp": w_spatial.reshape(DEPTH, SGU_GROUPS * SGU_CHUNK, SGU_CHUNK).astype(BF16),
        "bsp": jnp.repeat(jnp.swapaxes(b_spatial, 1, 2), BRANCH_W // SGU_GROUPS, axis=2),
        "dft_ctx": jnp.asarray(_dft_tables(SEQ)).astype(BF16),
        "dft_lat": jnp.asarray(_dft_tables(DEC_SEQ)).astype(BF16),
        "bdc": jnp.asarray(bdc).astype(BF16),
        "bds": jnp.asarray(bds).astype(BF16),
        "rope_cb": jnp.asarray(cb), "rope_sb": jnp.asarray(sb),
        "rope_cc": jnp.asarray(cc), "rope_sc": jnp.asarray(sc),
    }
    lat_caches = (
        cache_gqa_k.reshape(DEC_BATCH, DEPTH, PAST_LEN, 128),
        cache_gqa_v.reshape(DEC_BATCH, DEPTH, PAST_LEN, 128),
        cache_diff_k.reshape(DEC_BATCH, DEPTH, PAST_LEN, 256),
        cache_diff_v.reshape(DEC_BATCH, DEPTH, PAST_LEN, 256),
    )

    cond = jnp.concatenate(
        [c_ctx[None, :], c, jnp.zeros((COND_ROWS - 1 - DEC_BATCH, D_MODEL), F32)], axis=0)
    ada = (cond, w_ada, b_ada.reshape(DEPTH, 1, N_MOD * D_MODEL))
    mods = _ada_call(*ada).reshape(COND_ROWS, N_MOD, D_MODEL)

    merge_casts = {"w_in": w_in, "w_gate": w_gate}
    mlp_casts = {"w_branch": w_branch.reshape(DEPTH, N_BRANCH * BRANCH_W, D_MODEL), "w_out": w_out,
                 "w_mlp1": w_mlp1, "w_mlp2": w_mlp2}
    w = {k: v[0].astype(BF16) for k, v in {**merge_casts, **mlp_casts}.items()}

    x_ctx, x_lat, lat_row0 = x_prompt.reshape(CTX_ROWS, D_MODEL), x_sample.reshape(LAT_ROWS, D_MODEL), 0
    new_caches = None
    for layer in range(DEPTH):
        last = layer + 1 == DEPTH
        br_ctx, new_caches = _ctx_call(layer, x_ctx, mods, p, w, new_caches)
        pre = _lat_pre_call(layer, x_lat, lat_row0, mods, p, w)
        br_lat = _lat_mix_call(layer, pre, lat_caches, p)
        x, cast_a = _merge_call(layer, x_ctx, x_lat, lat_row0, br_ctx, br_lat, mods, p, w,
                                [] if last else list(merge_casts.values()))
        if last:
            y_ctx, y_lat = _mlp_call(layer, x, mods, p, w, final_g=final_norm_g.reshape(1, D_MODEL))
        else:
            outs = _mlp_call(layer, x, mods, p, w, list(mlp_casts.values()), ada)
            x, mods, cast_b = outs[0], outs[1].reshape(COND_ROWS, N_MOD, D_MODEL), outs[2:]
            w = dict(zip(list(merge_casts) + list(mlp_casts), list(cast_a) + list(cast_b)))
            x_ctx, x_lat, lat_row0 = x, x, CTX_ROWS

    gk, gv, dk, dv = new_caches
    return (
        y_ctx.reshape(BATCH, SEQ, D_MODEL),
        y_lat.reshape(DEC_BATCH, DEC_SEQ, D_MODEL),
        gk.reshape(BATCH, DEPTH, SEQ, GQA_KV_HEADS, GQA_HEAD_DIM),
        gv.reshape(BATCH, DEPTH, SEQ, GQA_KV_HEADS, GQA_HEAD_DIM),
        dk.reshape(BATCH, DEPTH, SEQ, DIFF_HEADS, 2, DIFF_QK_DIM),
        dv.reshape(BATCH, DEPTH, SEQ, DIFF_HEADS, DIFF_V_DIM),
    )
```

```python
import functools
import math

import numpy as np
import jax
import jax.numpy as jnp
from jax import lax
from jax.experimental import pallas as pl
from jax.experimental.pallas import tpu as pltpu

F32 = jnp.float32
BF16 = jnp.bfloat16

D_MODEL = 1024
BATCH = 32
SEQ = 256
DEPTH = 4
DEC_BATCH = 2
DEC_SEQ = 2048
PAST_LEN = 512
GRID_W = 64
ROPE_THETA = 10000.0
EPS = 1e-6
N_BRANCH = 4
BRANCH_W = 256
FNET_GW = 64
GQA_HEADS = 4
GQA_KV_HEADS = 2
GQA_HEAD_DIM = 64
DIFF_HEADS = 4
DIFF_V_DIM = 64
DIFF_QK_DIM = 32
SGU_GROUPS = 4
SGU_CHUNK = 128
D_FF = 4 * D_MODEL
N_MOD = 6
IN_W = 2048

CTX_ROWS = BATCH * SEQ
LAT_ROWS = DEC_BATCH * DEC_SEQ
ALL_ROWS = CTX_ROWS + LAT_ROWS
KEYS_LAT = PAST_LEN + DEC_SEQ

LANES = 128
VMEM_LIMIT = 52 * 1024 * 1024

CTX_SEQS_PER_STEP = 4
LAT_PRE_TM = 512
LAT_TQ = 256
KEY_CHUNK = 256
ROW_TM = 512
COND_ROWS = 8
CAST_STEPS = 16

_A0, _QB0, _KB0, _VB0, _QC0, _KC0, _VC0, _DU0, _DV0 = 0, 256, 512, 640, 768, 1024, 1280, 1536, 1792
_Q_HEAD_ORDER = (0, 2, 1, 3)
_LOG2E = math.log2(math.e)
_GQA_Q_SCALE = _LOG2E / math.sqrt(GQA_HEAD_DIM)
_DIFF_Q_SCALE = _LOG2E / math.sqrt(DIFF_QK_DIM)


def _dft_tables(length):
    k = np.arange(length, dtype=np.int64)
    idx = (k[:, None] * k[None, :]) % length
    ang = 2.0 * np.pi * idx.astype(np.float64) / length
    return np.concatenate([np.cos(ang), -np.sin(ang)], axis=1).astype(np.float32)


def _channel_dft_tables():
    k = np.arange(FNET_GW, dtype=np.int64)
    idx = (k[:, None] * k[None, :]) % FNET_GW
    ang = 2.0 * np.pi * idx.astype(np.float64) / FNET_GW
    c = np.cos(ang) / math.sqrt(FNET_GW)
    s = np.sin(ang) / math.sqrt(FNET_GW)
    eye = np.eye(BRANCH_W // FNET_GW)
    return (np.kron(eye, c).astype(np.float32), np.kron(eye, s).astype(np.float32))


def _rope_tables(length, dim):
    rows = length // GRID_W
    row = np.repeat(np.arange(rows, dtype=np.float64), GRID_W)
    col = np.tile(np.arange(GRID_W, dtype=np.float64), rows)
    quarter = dim // 4
    inv = ROPE_THETA ** (-np.arange(quarter, dtype=np.float64) / quarter)
    ang = np.concatenate([row[:, None] * inv, col[:, None] * inv], axis=-1)
    c = np.concatenate([np.cos(ang), np.cos(ang)], axis=-1)
    s = np.concatenate([-np.sin(ang), np.sin(ang)], axis=-1)
    reps = LANES // dim
    return (np.tile(c, (1, reps)).astype(np.float32), np.tile(s, (1, reps)).astype(np.float32))


def _dot(a, b):
    return jnp.dot(a, b, preferred_element_type=F32)


def _rms_mod(x, g, scale, shift, out_scale=1.0):
    y = x * lax.rsqrt(jnp.mean(x * x, axis=-1, keepdims=True) + EPS)
    return (y * (g * (1.0 + scale) * out_scale) + shift * out_scale).astype(BF16)


def _gelu(x):
    c = math.sqrt(2.0 / math.pi)
    return 0.5 * x * (1.0 + jnp.tanh(c * (x + 0.044715 * (x * x * x))))


def _group_ones(width, group):
    r = lax.broadcasted_iota(jnp.int32, (width, width), 0) // group
    c = lax.broadcasted_iota(jnp.int32, (width, width), 1) // group
    return (r == c).astype(BF16)


def _group_mean_sq(x, group):
    x2 = x * x
    hi = x2.astype(BF16)
    lo = (x2 - hi.astype(F32)).astype(BF16)
    ones = _group_ones(x.shape[-1], group)
    return (_dot(hi, ones) + _dot(lo, ones)) * (1.0 / group)


def _lane_mask(lo, hi):
    lane = lax.broadcasted_iota(jnp.int32, (1, LANES), 1)
    return (lane >= lo) & (lane < hi)


def _swap_halves(x, dim):
    n = x.shape[-1]
    half = dim // 2
    lane = lax.broadcasted_iota(jnp.int32, (1, n), 1)
    fwd = pltpu.roll(x, n - half, 1)
    bwd = pltpu.roll(x, half, 1)
    return jnp.where((lane % dim) < half, fwd, bwd)


def _rope(x, cos, sin, dim):
    reps = x.shape[-1] // LANES
    if reps > 1:
        cos = jnp.concatenate([cos] * reps, axis=1)
        sin = jnp.concatenate([sin] * reps, axis=1)
    return x * cos + _swap_halves(x, dim) * sin


def _pair_q_heads(q):
    b0, b1 = q[:, :LANES], q[:, LANES:]
    lo = _lane_mask(0, GQA_HEAD_DIM)
    r0 = pltpu.roll(b0, GQA_HEAD_DIM, 1)
    r1 = pltpu.roll(b1, GQA_HEAD_DIM, 1)
    return jnp.concatenate([jnp.where(lo, b0, r1), jnp.where(lo, r0, b1)], axis=1)


def _fold8(x, op):
    return op(x.reshape(x.shape[0] // 8, 8, x.shape[1]), axis=0)


class _Unit:
    def __init__(self, k_ref, k_lanes, qt_blk, masks, vt_ref, v_rows, out_row0, is_diff):
        self.k_ref, self.k_lanes, self.vt_ref, self.v_rows = k_ref, k_lanes, vt_ref, v_rows
        self.out_row0, self.is_diff = out_row0, is_diff
        row = lax.broadcasted_iota(jnp.int32, (LANES, 1), 0)
        self.qm = [jnp.where((row >= lo) & (row < hi), qt_blk, jnp.zeros_like(qt_blk))
                   for lo, hi in masks]
        self.m = self.scale = None


def _make_units(qbt, qct, kb_ref, kc_ref, vbt_ref, vct_ref):
    gqa_units, diff_units = [], []
    for pos, head in enumerate(_Q_HEAD_ORDER):
        blk, kv = pos // 2, pos % 2
        rows = slice(kv * GQA_HEAD_DIM, (kv + 1) * GQA_HEAD_DIM)
        gqa_units.append(_Unit(kb_ref, slice(0, LANES), qbt[blk * LANES:(blk + 1) * LANES, :],
                               [(rows.start, rows.stop)], vbt_ref, rows, head * GQA_HEAD_DIM, False))
    for head in range(DIFF_HEADS):
        blk, base = head // 2, (head % 2) * DIFF_V_DIM
        lanes = slice(blk * LANES, (blk + 1) * LANES)
        diff_units.append(_Unit(kc_ref, lanes, qct[lanes, :],
                                [(base, base + DIFF_QK_DIM), (base + DIFF_QK_DIM, base + DIFF_V_DIM)],
                                vct_ref, slice(head * DIFF_V_DIM, (head + 1) * DIFF_V_DIM),
                                BRANCH_W + head * DIFF_V_DIM, True))
    return [u for pair in zip(gqa_units, diff_units) for u in pair]


def _attention_pipeline(units, lam, s_buf, ot_ref):
    n = len(units)
    tq = s_buf.shape[-1]
    for j in range(n + 2):
        ua = units[j] if j < n else None
        ub = units[j - 1] if 1 <= j <= n else None
        uc = units[j - 2] if 2 <= j <= n + 1 and units[j - 2].is_diff else None
        sa, sb, sc = j % 3, (j - 1) % 3, (j - 2) % 3
        init = {}
        if ua is not None:
            init["mx"] = [jnp.full((8, tq), -jnp.inf, F32) for _ in ua.qm]
        if ub is not None:
            init["l"] = [jnp.zeros((8, tq), F32) for _ in ub.qm]
            if not ub.is_diff:
                init["ob"] = jnp.zeros((DIFF_V_DIM, tq), F32)
        if uc is not None:
            init["oc"] = jnp.zeros((DIFF_V_DIM, tq), F32)

        def body(c, carry, ua=ua, ub=ub, uc=uc, sa=sa, sb=sb, sc=sc):
            new = dict(carry)
            if ua is not None:
                k_c = ua.k_ref[c, :, ua.k_lanes]
                mx = []
                for mi, qm in enumerate(ua.qm):
                    s = _dot(k_c, qm)
                    s_buf[sa, mi, c] = s
                    mx.append(jnp.maximum(carry["mx"][mi], _fold8(s, jnp.max)))
                new["mx"] = mx
            if ub is not None:
                ls = []
                for mi in range(len(ub.qm)):
                    e = jnp.exp2(s_buf[sb, mi, c] - ub.m[mi])
                    ls.append(carry["l"][mi] + _fold8(e, jnp.sum))
                    if ub.is_diff:
                        s_buf[sb, mi, c] = e
                    else:
                        new["ob"] = carry["ob"] + _dot(ub.vt_ref[c, ub.v_rows, :], e.astype(BF16))
                new["l"] = ls
            if uc is not None:
                w = s_buf[sc, 0, c] - s_buf[sc, 1, c] * uc.scale[1]
                new["oc"] = carry["oc"] + _dot(uc.vt_ref[c, uc.v_rows, :], w.astype(BF16))
            return new

        out = lax.fori_loop(0, s_buf.shape[2], body, init, unroll=True)
        if ua is not None:
            ua.m = [jnp.max(mx, axis=0, keepdims=True) for mx in out["mx"]]
        if ub is not None:
            l = [jnp.sum(x, axis=0, keepdims=True) for x in out["l"]]
            if ub.is_diff:
                ub.scale = [1.0 / l[0], lam * l[0] / l[1]]
            else:
                ot_ref[ub.out_row0:ub.out_row0 + DIFF_V_DIM, :] = out["ob"] * (1.0 / l[0])
        if uc is not None:
            o = out["oc"] * uc.scale[0]
            ot_ref[uc.out_row0:uc.out_row0 + DIFF_V_DIM, :] = (
                o * lax.rsqrt(jnp.mean(o * o, axis=0, keepdims=True) + EPS))


def _lambda(lamv, layer):
    lam_init = 0.8 - 0.6 * math.exp(-0.3 * layer)
    a = jnp.sum(lamv[0:1] * lamv[1:2], axis=-1, keepdims=True)
    b = jnp.sum(lamv[2:3] * lamv[3:4], axis=-1, keepdims=True)
    return jnp.exp(a) - jnp.exp(b) + lam_init, 1.0 - lam_init


def _sgu(du, dvn, wsp, bias):
    lane = lax.broadcasted_iota(jnp.int32, (1, BRANCH_W), 1)
    outs = []
    for n in range(du.shape[0] // SGU_CHUNK):
        rows = slice(n * SGU_CHUNK, (n + 1) * SGU_CHUNK)
        r = _dot(wsp, dvn[rows].astype(BF16))
        s = bias
        for g in range(SGU_GROUPS):
            gm = (lane >= g * 64) & (lane < (g + 1) * 64)
            s = s + jnp.where(gm, r[g * SGU_CHUNK:(g + 1) * SGU_CHUNK], 0.0)
        outs.append(du[rows] * s)
    return outs[0] if len(outs) == 1 else jnp.concatenate(outs, axis=0)


def _fourier_stage1(a, bdc, bds):
    return jnp.concatenate([_dot(a, bdc), _dot(a, bds)], axis=0).astype(BF16)


def _fourier_stage2(dft_rows, f, length, w_fourier):
    y = _dot(dft_rows, f) * (1.0 / math.sqrt(length))
    return _dot(y.astype(BF16), w_fourier)


def _ada_kernel(cond_ref, w_ref, b_ref, o_ref):
    c = cond_ref[...]
    s = c * jax.nn.sigmoid(c)
    o_ref[...] = _dot(s.astype(BF16), w_ref[...].astype(BF16)) + b_ref[...]


def _ctx_kernel(layer, x_ref, mod_ref, g1_ref, win_ref, gq_ref, gk_ref, gsgu_ref, gdiff_ref,
                lamv_ref, dft_ref, bdc_ref, bds_ref, wf_ref, wsp_ref, bsp_ref,
                br_ref, ok_ref, ov_ref, odk_ref, odv_ref,
                kb_s, vbt_s, kc_s, vct_s, s_buf, ot_s):
    x = x_ref[...]
    mod = mod_ref[...]
    h = _rms_mod(x, g1_ref[...], mod[1:2], mod[0:1])
    proj = _dot(h, win_ref[...])
    qb = proj[:, _QB0:_KB0]
    kb = proj[:, _KB0:_VB0]
    qb = _pair_q_heads(qb * lax.rsqrt(_group_mean_sq(qb, GQA_HEAD_DIM) + EPS) * gq_ref[...])
    kb = kb * lax.rsqrt(_group_mean_sq(kb, GQA_HEAD_DIM) + EPS) * gk_ref[...]
    vb = proj[:, _VB0:_QC0]
    qc = proj[:, _QC0:_KC0]
    kc = proj[:, _KC0:_VC0]
    vc = proj[:, _VC0:_DU0]
    du = _gelu(proj[:, _DU0:_DV0])
    dv = _gelu(proj[:, _DV0:IN_W])
    dvn = dv * lax.rsqrt(jnp.mean(dv * dv, axis=-1, keepdims=True) + EPS) * gsgu_ref[...]
    lam, lam_scale = _lambda(lamv_ref[...], layer)
    qb = qb * _GQA_Q_SCALE
    qc = qc * _DIFF_Q_SCALE
    a = proj[:, _A0:_QB0].astype(BF16)
    for s in range(CTX_SEQS_PER_STEP):
        rows = slice(s * SEQ, (s + 1) * SEQ)
        ok_ref[s] = kb[rows]
        ov_ref[s] = vb[rows]
        odk_ref[s] = kc[rows]
        odv_ref[s] = vc[rows]
        kb_s[s, 0] = kb[rows].astype(BF16)
        kc_s[s, 0] = kc[rows].astype(BF16)
        vbt_s[s, 0] = vb[rows].T.astype(BF16)
        vct_s[s, 0] = vc[rows].T.astype(BF16)
        f = _fourier_stage1(a[rows], bdc_ref[...], bds_ref[...])
        br_ref[rows, 0:256] = _fourier_stage2(dft_ref[...], f, SEQ, wf_ref[...]).astype(BF16)
        br_ref[rows, 768:1024] = _sgu(du[rows], dvn[rows], wsp_ref[...], bsp_ref[...]).astype(BF16)
    for s in range(CTX_SEQS_PER_STEP):
        rows = slice(s * SEQ, (s + 1) * SEQ)
        units = _make_units(qb[rows].T.astype(BF16), qc[rows].T.astype(BF16),
                            kb_s.at[s], kc_s.at[s], vbt_s.at[s], vct_s.at[s])
        _attention_pipeline(units, lam, s_buf.at[s], ot_s.at[s])
        br_ref[rows, 256:512] = ot_s[s, 0:BRANCH_W, :].T.astype(BF16)
        br_ref[rows, 512:768] = (ot_s[s, BRANCH_W:2 * BRANCH_W, :].T
                                 * (gdiff_ref[...] * lam_scale)).astype(BF16)


def _lat_pre_kernel(x_ref, mod_ref, g1_ref, win_ref, gq_ref, gk_ref, gsgu_ref,
                    cb_ref, sb_ref, cc_ref, sc_ref,
                    a_ref, qb_ref, kb_ref, vbt_ref, qc_ref, kc_ref, vct_ref, du_ref, dvn_ref):
    x = x_ref[...]
    mod = mod_ref[...]
    h = _rms_mod(x, g1_ref[...], mod[1:2], mod[0:1])
    cb, sb, cc, sc = cb_ref[...], sb_ref[...], cc_ref[...], sc_ref[...]
    proj = lambda c0, c1: _dot(h, win_ref[:, c0:c1])
    qb = proj(_QB0, _KB0)
    kvb = proj(_KB0, _QC0)
    kb, vb = kvb[:, :LANES], kvb[:, LANES:]
    qb = qb * lax.rsqrt(_group_mean_sq(qb, GQA_HEAD_DIM) + EPS) * gq_ref[...]
    kb = kb * lax.rsqrt(_group_mean_sq(kb, GQA_HEAD_DIM) + EPS) * gk_ref[...]
    qb_ref[...] = _pair_q_heads(_rope(qb, cb, sb, GQA_HEAD_DIM) * _GQA_Q_SCALE).T.astype(BF16)
    kb_ref[...] = _rope(kb, cb, sb, GQA_HEAD_DIM).astype(BF16)
    vbt_ref[...] = vb.T.astype(BF16)
    qc = proj(_QC0, _KC0)
    kc = proj(_KC0, _VC0)
    qc_ref[...] = (_rope(qc, cc, sc, DIFF_QK_DIM) * _DIFF_Q_SCALE).T.astype(BF16)
    kc_ref[...] = _rope(kc, cc, sc, DIFF_QK_DIM).astype(BF16)
    vc = proj(_VC0, _DU0)
    du = proj(_DU0, _DV0)
    vct_ref[...] = vc.T.astype(BF16)
    du_ref[...] = _gelu(du)
    dv = _gelu(proj(_DV0, IN_W))
    a_ref[...] = proj(_A0, _QB0).astype(BF16)
    dvn_ref[...] = (dv * lax.rsqrt(jnp.mean(dv * dv, axis=-1, keepdims=True) + EPS)
                    * gsgu_ref[...]).astype(BF16)


def _lat_mix_kernel(layer, a_ref, qb_ref, kb_ref, vbt_ref, qc_ref, kc_ref, vct_ref,
                    du_ref, dvn_ref, ck_ref, cv_ref, cdk_ref, cdv_ref, gdiff_ref, lamv_ref,
                    dft_ref, bdc_ref, bds_ref, wf_ref, wsp_ref, bsp_ref,
                    br_ref, kb_s, vbt_s, kc_s, vct_s, f_s, s_buf, ot_s):
    kc = KEY_CHUNK
    past_chunks = PAST_LEN // kc

    @pl.when(pl.program_id(1) == 0)
    def _fill():
        for c in range(KEYS_LAT // kc):
            if c < past_chunks:
                rows = slice(c * kc, (c + 1) * kc)
                kb_s[c] = ck_ref[rows, :].astype(BF16)
                kc_s[c] = cdk_ref[rows, :].astype(BF16)
                vbt_s[c] = cv_ref[rows, :].T.astype(BF16)
                vct_s[c] = cdv_ref[rows, :].T.astype(BF16)
            else:
                rows = slice((c - past_chunks) * kc, (c - past_chunks + 1) * kc)
                kb_s[c] = kb_ref[rows, :]
                kc_s[c] = kc_ref[rows, :]
                vbt_s[c] = vbt_ref[:, rows]
                vct_s[c] = vct_ref[:, rows]
        f_s[...] = _fourier_stage1(a_ref[...], bdc_ref[...], bds_ref[...])

    lam, lam_scale = _lambda(lamv_ref[...], layer)
    units = _make_units(qb_ref[...], qc_ref[...], kb_s, kc_s, vbt_s, vct_s)
    _attention_pipeline(units, lam, s_buf, ot_s)
    br_ref[:, 0:256] = _fourier_stage2(dft_ref[...], f_s[...], DEC_SEQ, wf_ref[...]).astype(BF16)
    br_ref[:, 768:1024] = _sgu(du_ref[...], dvn_ref[...].astype(F32), wsp_ref[...],
                               bsp_ref[...]).astype(BF16)
    br_ref[:, 256:512] = ot_s[0:BRANCH_W, :].T.astype(BF16)
    br_ref[:, 512:768] = (ot_s[BRANCH_W:2 * BRANCH_W, :].T * (gdiff_ref[...] * lam_scale)).astype(BF16)


def _cast_slabs(n_cast, refs, step=None, steps=CAST_STEPS):
    step = pl.program_id(0) if step is None else step

    @pl.when(step < steps)
    def _():
        for src, dst in zip(refs[:n_cast], refs[n_cast:]):
            dst[...] = src[...].astype(BF16)


def _merge_kernel(n_cast, xc_ref, xl_ref, brc_ref, brl_ref, mod_ref, g1_ref, wg_ref, wb_ref, wo_ref,
                  *rest):
    cast_src, o_ref, cast_dst = rest[:n_cast], rest[n_cast], rest[n_cast + 1:]
    is_ctx = pl.program_id(0) < CTX_ROWS // ROW_TM
    x = jnp.where(is_ctx, xc_ref[...], xl_ref[...])
    br = jnp.where(is_ctx, brc_ref[...], brl_ref[...])
    mod = mod_ref[...]
    h_half = _rms_mod(x, g1_ref[...], mod[1:2], mod[0:1], out_scale=0.5)
    br_half = br * 0.5
    merged = None
    for n in range(N_BRANCH):
        t = jnp.tanh(_dot(h_half, wg_ref[:, n * D_MODEL:(n + 1) * D_MODEL]))
        term = (t + 1.0) * _dot(br_half[:, n * BRANCH_W:(n + 1) * BRANCH_W],
                                wb_ref[n * BRANCH_W:(n + 1) * BRANCH_W, :])
        merged = term if merged is None else merged + term
    o_ref[...] = x + mod[2:3] * _dot(merged.astype(BF16), wo_ref[...])
    _cast_slabs(n_cast, cast_src + cast_dst)


def _mlp_block(x, mod, g2, w1_ref, w2_ref):
    h = _rms_mod(x, g2, mod[4:5], mod[3:4])
    acc = None
    for n in range(D_FF // D_MODEL):
        cols = slice(n * D_MODEL, (n + 1) * D_MODEL)
        hid = jnp.maximum(_dot(h, w1_ref[:, cols]), 0.0)
        term = _dot((hid * hid).astype(BF16), w2_ref[cols, :])
        acc = term if acc is None else acc + term
    return x + mod[5:6] * acc


def _mlp_kernel(n_cast, x_ref, mod_ref, g2_ref, w1_ref, w2_ref, cond_ref, wada_ref, bada_ref, *rest):
    cast_src, o_ref, modn_ref, cast_dst = rest[:n_cast], rest[n_cast], rest[n_cast + 1], rest[n_cast + 2:]
    o_ref[...] = _mlp_block(x_ref[...], mod_ref[...], g2_ref[...], w1_ref, w2_ref)
    _ada_kernel(cond_ref, wada_ref, bada_ref, modn_ref)
    _cast_slabs(n_cast, cast_src + cast_dst)


def _mlp_final_kernel(x_ref, mod_ref, g2_ref, w1_ref, w2_ref, gf_ref, yc_ref, yl_ref):
    x = _mlp_block(x_ref[...], mod_ref[...], g2_ref[...], w1_ref, w2_ref)
    y = x * lax.rsqrt(jnp.mean(x * x, axis=-1, keepdims=True) + EPS) * gf_ref[...]
    is_ctx = pl.program_id(0) < CTX_ROWS // ROW_TM

    @pl.when(is_ctx)
    def _():
        yc_ref[...] = y

    @pl.when(jnp.logical_not(is_ctx))
    def _():
        yl_ref[...] = y


def _params(*sem):
    return pltpu.CompilerParams(dimension_semantics=sem, vmem_limit_bytes=VMEM_LIMIT)


def _full(shape):
    nd = len(shape)
    return pl.BlockSpec(shape, lambda *_: (0,) * nd)


def _layer_block(shape, layer):
    nd = len(shape)
    return pl.BlockSpec((None,) + tuple(shape), lambda *_: (layer,) + (0,) * nd)


def _resident(shape):
    nd = len(shape)
    return pl.BlockSpec(shape, lambda *_: (0,) * nd, pipeline_mode=pl.Buffered(1))


def _cast_job(w, layer, steps=CAST_STEPS, step_of=lambda i: i):
    _, r, c = w.shape
    slab = r // steps
    step = lambda *idx: jnp.minimum(step_of(*idx), steps - 1)
    return (pl.BlockSpec((None, slab, c), lambda *idx: (layer, step(*idx), 0)),
            pl.BlockSpec((slab, c), lambda *idx: (step(*idx), 0)),
            jax.ShapeDtypeStruct((r, c), BF16))


def _mod_spec(select):
    return pl.BlockSpec((None, N_MOD, D_MODEL), lambda *idx: (select(*idx), 0, 0))


def _row_mod_select(tm):
    ctx_tiles = CTX_ROWS // tm
    tiles_per_seq = DEC_SEQ // tm
    return lambda i: jnp.where(i < ctx_tiles, 0, 1 + (i - ctx_tiles) // tiles_per_seq)


def _ada_call(cond, w_ada, b_ada):
    return pl.pallas_call(
        _ada_kernel,
        grid=(N_MOD,),
        in_specs=[
            pl.BlockSpec((COND_ROWS, D_MODEL), lambda j: (0, 0)),
            pl.BlockSpec((None, D_MODEL, D_MODEL), lambda j: (0, 0, j)),
            pl.BlockSpec((None, 1, D_MODEL), lambda j: (0, 0, j)),
        ],
        out_specs=pl.BlockSpec((COND_ROWS, D_MODEL), lambda j: (0, j)),
        out_shape=jax.ShapeDtypeStruct((COND_ROWS, N_MOD * D_MODEL), F32),
        compiler_params=_params("arbitrary"),
        name="ada",
    )(cond, w_ada, b_ada)


def _ctx_call(layer, x, mods, p, w, caches, casts=()):
    ns = CTX_SEQS_PER_STEP
    tm = ns * SEQ
    cache_shapes = [(BATCH, DEPTH, SEQ, 128), (BATCH, DEPTH, SEQ, 128),
                    (BATCH, DEPTH, SEQ, 256), (BATCH, DEPTH, SEQ, 256)]
    in_specs = [
        pl.BlockSpec((tm, D_MODEL), lambda i: (i, 0)),
        _mod_spec(lambda i: 0),
        _layer_block((1, D_MODEL), layer),
        _resident((D_MODEL, IN_W)),
        _layer_block((1, 256), layer),
        _layer_block((1, 128), layer),
        _layer_block((1, 256), layer),
        _layer_block((1, 256), layer),
        _layer_block((4, DIFF_QK_DIM), layer),
        _full((SEQ, 2 * SEQ)),
        _full((BRANCH_W, BRANCH_W)),
        _full((BRANCH_W, BRANCH_W)),
        _layer_block((BRANCH_W, BRANCH_W), layer),
        _layer_block((SGU_GROUPS * SGU_CHUNK, SGU_CHUNK), layer),
        _layer_block((SGU_CHUNK, BRANCH_W), layer),
    ]
    args = [x, mods, p["norm1_g"], w["w_in"], p["gq"], p["gk"], p["gsgu"], p["gdiff"], p["lamv"],
            p["dft_ctx"], p["bdc"], p["bds"], p["w_fourier"], p["wsp"], p["bsp"]]
    n_base = len(args)
    steps = BATCH // ns
    jobs = [_cast_job(c, layer, steps) for c in casts]
    in_specs += [j[0] for j in jobs]
    args += list(casts)
    aliases = {}
    n_alias = 0
    if caches is not None:
        n_alias = len(caches)
        for j, cache in enumerate(caches):
            in_specs.append(pl.BlockSpec(memory_space=pl.ANY))
            aliases[len(args)] = 1 + j
            args.append(cache)
    n_cast, n_out = len(jobs), 1 + len(cache_shapes)

    def kernel(*refs):
        base_in, cast_src = refs[:n_base], refs[n_base:n_base + n_cast]
        rest = refs[n_base + n_cast + n_alias:]
        outs, cast_dst, scratch = rest[:n_out], rest[n_out:n_out + n_cast], rest[n_out + n_cast:]
        _ctx_kernel(layer, *base_in, *outs, *scratch)
        if n_cast:
            _cast_slabs(n_cast, cast_src + cast_dst, steps=steps)

    out_specs = [pl.BlockSpec((tm, D_MODEL), lambda i: (i, 0))] + [
        pl.BlockSpec((ns, None, SEQ, s[-1]), lambda i: (i, layer, 0, 0)) for s in cache_shapes]
    out_shape = [jax.ShapeDtypeStruct((CTX_ROWS, D_MODEL), BF16)] + [
        jax.ShapeDtypeStruct(s, F32) for s in cache_shapes]
    out_specs += [j[1] for j in jobs]
    out_shape += [j[2] for j in jobs]
    outs = pl.pallas_call(
        kernel,
        grid=(steps,),
        in_specs=in_specs,
        out_specs=out_specs,
        out_shape=out_shape,
        scratch_shapes=[
            pltpu.VMEM((ns, 1, SEQ, 128), BF16), pltpu.VMEM((ns, 1, 128, SEQ), BF16),
            pltpu.VMEM((ns, 1, SEQ, 256), BF16), pltpu.VMEM((ns, 1, 256, SEQ), BF16),
            pltpu.VMEM((ns, 3, 2, 1, SEQ, SEQ), F32),
            pltpu.VMEM((ns, 2 * BRANCH_W, SEQ), F32),
        ],
        input_output_aliases=aliases,
        compiler_params=_params("arbitrary"),
        name=f"ctx_mix_{layer}",
    )(*args)
    return outs[0], tuple(outs[1:n_out]), list(outs[n_out:])


def _lat_pre_call(layer, x, row0, mods, p, w):
    tm = LAT_PRE_TM
    tile0 = row0 // tm
    tiles_per_seq = DEC_SEQ // tm
    rope_spec = pl.BlockSpec((tm, LANES), lambda i: (i % tiles_per_seq, 0))
    rows = lambda w, dt: (pl.BlockSpec((tm, w), lambda i: (i, 0)),
                          jax.ShapeDtypeStruct((LAT_ROWS, w), dt))
    cols = lambda w: (pl.BlockSpec((None, w, tm), lambda i: (i // tiles_per_seq, 0, i % tiles_per_seq)),
                      jax.ShapeDtypeStruct((DEC_BATCH, w, DEC_SEQ), BF16))
    outs = [rows(256, BF16), cols(256), rows(128, BF16), cols(128),
            cols(256), rows(256, BF16), cols(256), rows(256, F32), rows(256, BF16)]
    return pl.pallas_call(
        _lat_pre_kernel,
        grid=(LAT_ROWS // tm,),
        in_specs=[
            pl.BlockSpec((tm, D_MODEL), lambda i: (tile0 + i, 0)),
            _mod_spec(lambda i: 1 + i // tiles_per_seq),
            _layer_block((1, D_MODEL), layer),
            _resident((D_MODEL, IN_W)),
            _layer_block((1, 256), layer),
            _layer_block((1, 128), layer),
            _layer_block((1, 256), layer),
            rope_spec, rope_spec, rope_spec, rope_spec,
        ],
        out_specs=[spec for spec, _ in outs],
        out_shape=[shape for _, shape in outs],
        compiler_params=_params("arbitrary"),
        name=f"lat_pre_{layer}",
    )(x, mods, p["norm1_g"], w["w_in"], p["gq"], p["gk"], p["gsgu"],
      p["rope_cb"], p["rope_sb"], p["rope_cc"], p["rope_sc"])


def _lat_mix_call(layer, pre, caches, p, casts=()):
    tq = LAT_TQ
    nq = DEC_SEQ // tq
    n_chunks = KEYS_LAT // KEY_CHUNK
    steps = DEC_BATCH * nq
    linear = lambda b, q: b * nq + q
    jobs = [_cast_job(c, layer, steps, linear) for c in casts]
    n_in, n_cast = 21, len(jobs)

    def kernel(*refs):
        base_in, cast_src = refs[:n_in], refs[n_in:n_in + n_cast]
        br_ref, cast_dst = refs[n_in + n_cast], refs[n_in + n_cast + 1:n_in + 2 * n_cast + 1]
        _lat_mix_kernel(layer, *base_in, br_ref, *refs[n_in + 2 * n_cast + 1:])
        if n_cast:
            _cast_slabs(n_cast, cast_src + cast_dst, step=linear(pl.program_id(0), pl.program_id(1)),
                        steps=steps)

    seq_block = lambda w: pl.BlockSpec((DEC_SEQ, w), lambda b, q: (b, 0))
    seq_block_t = lambda w: pl.BlockSpec((None, w, DEC_SEQ), lambda b, q: (b, 0, 0))
    q_block = lambda w: pl.BlockSpec((tq, w), lambda b, q: (b * nq + q, 0))
    q_block_t = lambda w: pl.BlockSpec((None, w, tq), lambda b, q: (b, 0, q))
    cache_block = lambda w: pl.BlockSpec((None, None, PAST_LEN, w), lambda b, q: (b, layer, 0, 0))
    outs = pl.pallas_call(
        kernel,
        grid=(DEC_BATCH, nq),
        in_specs=[
            seq_block(256), q_block_t(256), seq_block(128), seq_block_t(128),
            q_block_t(256), seq_block(256), seq_block_t(256),
            q_block(256), q_block(256),
            cache_block(128), cache_block(128), cache_block(256), cache_block(256),
            _layer_block((1, 256), layer),
            _layer_block((4, DIFF_QK_DIM), layer),
            pl.BlockSpec((tq, 2 * DEC_SEQ), lambda b, q: (q, 0)),
            _full((BRANCH_W, BRANCH_W)),
            _full((BRANCH_W, BRANCH_W)),
            _layer_block((BRANCH_W, BRANCH_W), layer),
            _layer_block((SGU_GROUPS * SGU_CHUNK, SGU_CHUNK), layer),
            _layer_block((SGU_CHUNK, BRANCH_W), layer),
        ] + [j[0] for j in jobs],
        out_specs=[q_block(D_MODEL)] + [j[1] for j in jobs],
        out_shape=[jax.ShapeDtypeStruct((LAT_ROWS, D_MODEL), BF16)] + [j[2] for j in jobs],
        scratch_shapes=[
            pltpu.VMEM((n_chunks, KEY_CHUNK, 128), BF16), pltpu.VMEM((n_chunks, 128, KEY_CHUNK), BF16),
            pltpu.VMEM((n_chunks, KEY_CHUNK, 256), BF16), pltpu.VMEM((n_chunks, 256, KEY_CHUNK), BF16),
            pltpu.VMEM((2 * DEC_SEQ, BRANCH_W), BF16),
            pltpu.VMEM((3, 2, n_chunks, KEY_CHUNK, tq), F32),
            pltpu.VMEM((2 * BRANCH_W, tq), F32),
        ],
        compiler_params=_params("arbitrary", "arbitrary"),
        name=f"lat_mix_{layer}",
    )(*pre, *caches, p["gdiff"], p["lamv"],
      p["dft_lat"], p["bdc"], p["bds"], p["w_fourier"], p["wsp"], p["bsp"], *casts)
    return outs[0], list(outs[1:])


def _merge_call(layer, x_ctx, x_lat, lat_row0, br_ctx, br_lat, mods, p, w, casts):
    tm = ROW_TM
    ctx_tiles = CTX_ROWS // tm
    lat_tile0 = lat_row0 // tm
    ctx_tile = lambda i: (jnp.minimum(i, ctx_tiles - 1), 0)
    jobs = [_cast_job(c, layer + 1) for c in casts]
    outs = pl.pallas_call(
        functools.partial(_merge_kernel, len(jobs)),
        grid=(ALL_ROWS // tm,),
        in_specs=[
            pl.BlockSpec((tm, D_MODEL), ctx_tile),
            pl.BlockSpec((tm, D_MODEL), lambda i: (lat_tile0 + jnp.maximum(i - ctx_tiles, 0), 0)),
            pl.BlockSpec((tm, D_MODEL), ctx_tile),
            pl.BlockSpec((tm, D_MODEL), lambda i: (jnp.maximum(i - ctx_tiles, 0), 0)),
            _mod_spec(_row_mod_select(tm)),
            _layer_block((1, D_MODEL), layer),
            _resident((D_MODEL, N_BRANCH * D_MODEL)),
            _resident((N_BRANCH * BRANCH_W, D_MODEL)),
            _resident((D_MODEL, D_MODEL)),
        ] + [j[0] for j in jobs],
        out_specs=[pl.BlockSpec((tm, D_MODEL), lambda i: (i, 0))] + [j[1] for j in jobs],
        out_shape=[jax.ShapeDtypeStruct((ALL_ROWS, D_MODEL), F32)] + [j[2] for j in jobs],
        compiler_params=_params("arbitrary"),
        name=f"merge_{layer}",
    )(x_ctx, x_lat, br_ctx, br_lat, mods, p["norm1_g"], w["w_gate"], w["w_branch"], w["w_out"], *casts)
    return outs[0], outs[1:]


def _mlp_call(layer, x, mods, p, w, casts=(), ada=None, final_g=None):
    tm = ROW_TM
    steps = ALL_ROWS // tm
    ctx_tiles = CTX_ROWS // tm
    in_specs = [
        pl.BlockSpec((tm, D_MODEL), lambda i: (i, 0)),
        _mod_spec(_row_mod_select(tm)),
        _layer_block((1, D_MODEL), layer),
        _resident((D_MODEL, D_FF)),
        _resident((D_FF, D_MODEL)),
    ]
    args = [x, mods, p["norm2_g"], w["w_mlp1"], w["w_mlp2"]]
    if final_g is None:
        ada_w = N_MOD * D_MODEL // steps
        jobs = [_cast_job(c, layer + 1) for c in casts]
        body = functools.partial(_mlp_kernel, len(jobs))
        in_specs += [
            _full((COND_ROWS, D_MODEL)),
            pl.BlockSpec((None, D_MODEL, ada_w), lambda i: (layer + 1, 0, i)),
            pl.BlockSpec((None, 1, ada_w), lambda i: (layer + 1, 0, i)),
        ] + [j[0] for j in jobs]
        args += list(ada) + list(casts)
        out_specs = [pl.BlockSpec((tm, D_MODEL), lambda i: (i, 0)),
                     pl.BlockSpec((COND_ROWS, ada_w), lambda i: (0, i))] + [j[1] for j in jobs]
        out_shape = [jax.ShapeDtypeStruct((ALL_ROWS, D_MODEL), F32),
                     jax.ShapeDtypeStruct((COND_ROWS, N_MOD * D_MODEL), F32)] + [j[2] for j in jobs]
    else:
        body = _mlp_final_kernel
        in_specs.append(_full((1, D_MODEL)))
        args.append(final_g)
        out_specs = [
            pl.BlockSpec((tm, D_MODEL), lambda i: (jnp.minimum(i, ctx_tiles - 1), 0)),
            pl.BlockSpec((tm, D_MODEL), lambda i: (jnp.maximum(i - ctx_tiles, 0), 0)),
        ]
        out_shape = [jax.ShapeDtypeStruct((CTX_ROWS, D_MODEL), F32),
                     jax.ShapeDtypeStruct((LAT_ROWS, D_MODEL), F32)]
    return pl.pallas_call(
        body,
        grid=(steps,),
        in_specs=in_specs,
        out_specs=out_specs,
        out_shape=out_shape,
        compiler_params=_params("arbitrary"),
        name=f"mlp_{layer}",
    )(*args)


def kernel(x_prompt, x_sample, c, cache_gqa_k, cache_gqa_v, cache_diff_k, cache_diff_v, c_ctx, w_ada, b_ada, norm1_g, norm2_g, w_in, w_fourier, q_norm_g, k_norm_g, lambda_q1, lambda_k1, lambda_q2, lambda_k2, diff_norm_g, sgu_norm_g, w_spatial, b_spatial, w_gate, w_branch, w_out, w_mlp1, w_mlp2, final_norm_g):
    bdc, bds = _channel_dft_tables()
    cb, sb = _rope_tables(DEC_SEQ, GQA_HEAD_DIM)
    cc, sc = _rope_tables(DEC_SEQ, DIFF_QK_DIM)
    p = {
        "norm1_g": norm1_g.reshape(DEPTH, 1, D_MODEL),
        "norm2_g": norm2_g.reshape(DEPTH, 1, D_MODEL),
        "w_fourier": w_fourier.astype(BF16),
        "gq": jnp.tile(q_norm_g, (1, GQA_HEADS)).reshape(DEPTH, 1, 256),
        "gk": jnp.tile(k_norm_g, (1, GQA_KV_HEADS)).reshape(DEPTH, 1, 128),
        "gsgu": sgu_norm_g.reshape(DEPTH, 1, BRANCH_W),
        "gdiff": jnp.tile(diff_norm_g, (1, DIFF_HEADS)).reshape(DEPTH, 1, 256),
        "lamv": jnp.stack([lambda_q1, lambda_k1, lambda_q2, lambda_k2], axis=1),
        "wsp": w_spatial.reshape(DEPTH, SGU_GROUPS * SGU_CHUNK, SGU_CHUNK).astype(BF16),
        "bsp": jnp.repeat(jnp.swapaxes(b_spatial, 1, 2), BRANCH_W // SGU_GROUPS, axis=2),
        "dft_ctx": jnp.asarray(_dft_tables(SEQ)).astype(BF16),
        "dft_lat": jnp.asarray(_dft_tables(DEC_SEQ)).astype(BF16),
        "bdc": jnp.asarray(bdc).astype(BF16),
        "bds": jnp.asarray(bds).astype(BF16),
        "rope_cb": jnp.asarray(cb), "rope_sb": jnp.asarray(sb),
        "rope_cc": jnp.asarray(cc), "rope_sc": jnp.asarray(sc),
    }
    lat_caches = (
        cache_gqa_k.reshape(DEC_BATCH, DEPTH, PAST_LEN, 128),
        cache_gqa_v.reshape(DEC_BATCH, DEPTH, PAST_LEN, 128),
        cache_diff_k.reshape(DEC_BATCH, DEPTH, PAST_LEN, 256),
        cache_diff_v.reshape(DEC_BATCH, DEPTH, PAST_LEN, 256),
    )

    cond = jnp.concatenate(
        [c_ctx[None, :], c, jnp.zeros((COND_ROWS - 1 - DEC_BATCH, D_MODEL), F32)], axis=0)
    ada = (cond, w_ada, b_ada.reshape(DEPTH, 1, N_MOD * D_MODEL))
    mods = _ada_call(*ada).reshape(COND_ROWS, N_MOD, D_MODEL)

    merge_casts = {"w_in": w_in, "w_gate": w_gate}
    mlp_casts = {"w_branch": w_branch.reshape(DEPTH, N_BRANCH * BRANCH_W, D_MODEL), "w_out": w_out,
                 "w_mlp1": w_mlp1, "w_mlp2": w_mlp2}
    w = {"w_in": w_in[0].astype(BF16)}
    big = {**merge_casts, **mlp_casts}
    ctx_casts = {k: big[k] for k in ("w_gate", "w_branch", "w_out")}
    lat_casts = {k: big[k] for k in ("w_mlp1", "w_mlp2")}

    x_ctx, x_lat, lat_row0 = x_prompt.reshape(CTX_ROWS, D_MODEL), x_sample.reshape(LAT_ROWS, D_MODEL), 0
    new_caches = None
    for layer in range(DEPTH):
        last = layer + 1 == DEPTH
        first = layer == 0
        br_ctx, new_caches, cast_c = _ctx_call(layer, x_ctx, mods, p, w, new_caches,
                                               list(ctx_casts.values()) if first else [])
        pre = _lat_pre_call(layer, x_lat, lat_row0, mods, p, w)
        br_lat, cast_l = _lat_mix_call(layer, pre, lat_caches, p,
                                       list(lat_casts.values()) if first else [])
        if first:
            w.update(zip(list(ctx_casts) + list(lat_casts), cast_c + cast_l))
        x, cast_a = _merge_call(layer, x_ctx, x_lat, lat_row0, br_ctx, br_lat, mods, p, w,
                                [] if last else list(merge_casts.values()))
        if last:
            y_ctx, y_lat = _mlp_call(layer, x, mods, p, w, final_g=final_norm_g.reshape(1, D_MODEL))
        else:
            outs = _mlp_call(layer, x, mods, p, w, list(mlp_casts.values()), ada)
            x, mods, cast_b = outs[0], outs[1].reshape(COND_ROWS, N_MOD, D_MODEL), outs[2:]
            w = dict(zip(list(merge_casts) + list(mlp_casts), list(cast_a) + list(cast_b)))
            x_ctx, x_lat, lat_row0 = x, x, CTX_ROWS

    gk, gv, dk, dv = new_caches
    return (
        y_ctx.reshape(BATCH, SEQ, D_MODEL),
        y_lat.reshape(DEC_BATCH, DEC_SEQ, D_MODEL),
        gk.reshape(BATCH, DEPTH, SEQ, GQA_KV_HEADS, GQA_HEAD_DIM),
        gv.reshape(BATCH, DEPTH, SEQ, GQA_KV_HEADS, GQA_HEAD_DIM),
        dk.reshape(BATCH, DEPTH, SEQ, DIFF_HEADS, 2, DIFF_QK_DIM),
        dv.reshape(BATCH, DEPTH, SEQ, DIFF_HEADS, DIFF_V_DIM),
    )
```

```python
import functools
import math

import numpy as np
import jax
import jax.numpy as jnp
from jax import lax
from jax.experimental import pallas as pl
from jax.experimental.pallas import tpu as pltpu

F32 = jnp.float32
BF16 = jnp.bfloat16

D_MODEL = 1024
BATCH = 32
SEQ = 256
DEPTH = 4
DEC_BATCH = 2
DEC_SEQ = 2048
PAST_LEN = 512
GRID_W = 64
ROPE_THETA = 10000.0
EPS = 1e-6
N_BRANCH = 4
BRANCH_W = 256
FNET_GW = 64
GQA_HEADS = 4
GQA_KV_HEADS = 2
GQA_HEAD_DIM = 64
DIFF_HEADS = 4
DIFF_V_DIM = 64
DIFF_QK_DIM = 32
SGU_GROUPS = 4
SGU_CHUNK = 128
D_FF = 4 * D_MODEL
N_MOD = 6
IN_W = 2048

CTX_ROWS = BATCH * SEQ
LAT_ROWS = DEC_BATCH * DEC_SEQ
ALL_ROWS = CTX_ROWS + LAT_ROWS
KEYS_LAT = PAST_LEN + DEC_SEQ

LANES = 128
VMEM_LIMIT = 52 * 1024 * 1024

CTX_SEQS_PER_STEP = 4
LAT_PRE_TM = 1024
LAT_TQ = 256
KEY_CHUNK = 256
ROW_TM = 512
MLP_TM = 1024
COND_ROWS = 8
CAST_STEPS = 16

_A0, _QB0, _KB0, _VB0, _QC0, _KC0, _VC0, _DU0, _DV0 = 0, 256, 512, 640, 768, 1024, 1280, 1536, 1792
_Q_HEAD_ORDER = (0, 2, 1, 3)
_LOG2E = math.log2(math.e)
_GQA_Q_SCALE = _LOG2E / math.sqrt(GQA_HEAD_DIM)
_DIFF_Q_SCALE = _LOG2E / math.sqrt(DIFF_QK_DIM)


def _dft_tables(length):
    k = np.arange(length, dtype=np.int64)
    idx = (k[:, None] * k[None, :]) % length
    ang = 2.0 * np.pi * idx.astype(np.float64) / length
    return np.concatenate([np.cos(ang), -np.sin(ang)], axis=1).astype(np.float32)


def _channel_dft_tables():
    k = np.arange(FNET_GW, dtype=np.int64)
    idx = (k[:, None] * k[None, :]) % FNET_GW
    ang = 2.0 * np.pi * idx.astype(np.float64) / FNET_GW
    c = np.cos(ang) / math.sqrt(FNET_GW)
    s = np.sin(ang) / math.sqrt(FNET_GW)
    eye = np.eye(BRANCH_W // FNET_GW)
    return (np.kron(eye, c).astype(np.float32), np.kron(eye, s).astype(np.float32))


def _rope_tables(length, dim):
    rows = length // GRID_W
    row = np.repeat(np.arange(rows, dtype=np.float64), GRID_W)
    col = np.tile(np.arange(GRID_W, dtype=np.float64), rows)
    quarter = dim // 4
    inv = ROPE_THETA ** (-np.arange(quarter, dtype=np.float64) / quarter)
    ang = np.concatenate([row[:, None] * inv, col[:, None] * inv], axis=-1)
    c = np.concatenate([np.cos(ang), np.cos(ang)], axis=-1)
    s = np.concatenate([-np.sin(ang), np.sin(ang)], axis=-1)
    reps = LANES // dim
    return (np.tile(c, (1, reps)).astype(np.float32), np.tile(s, (1, reps)).astype(np.float32))


def _dot(a, b):
    return jnp.dot(a, b, preferred_element_type=F32)


def _rms_mod(x, g, scale, shift, out_scale=1.0):
    y = x * lax.rsqrt(jnp.mean(x * x, axis=-1, keepdims=True) + EPS)
    return (y * (g * (1.0 + scale) * out_scale) + shift * out_scale).astype(BF16)


def _gelu(x):
    c = math.sqrt(2.0 / math.pi)
    return 0.5 * x * (1.0 + jnp.tanh(c * (x + 0.044715 * (x * x * x))))


def _group_ones(width, group):
    r = lax.broadcasted_iota(jnp.int32, (width, width), 0) // group
    c = lax.broadcasted_iota(jnp.int32, (width, width), 1) // group
    return (r == c).astype(BF16)


def _group_mean_sq(x, group):
    x2 = x * x
    hi = x2.astype(BF16)
    lo = (x2 - hi.astype(F32)).astype(BF16)
    ones = _group_ones(x.shape[-1], group)
    return (_dot(hi, ones) + _dot(lo, ones)) * (1.0 / group)


def _lane_mask(lo, hi):
    lane = lax.broadcasted_iota(jnp.int32, (1, LANES), 1)
    return (lane >= lo) & (lane < hi)


def _swap_halves(x, dim):
    n = x.shape[-1]
    half = dim // 2
    lane = lax.broadcasted_iota(jnp.int32, (1, n), 1)
    fwd = pltpu.roll(x, n - half, 1)
    bwd = pltpu.roll(x, half, 1)
    return jnp.where((lane % dim) < half, fwd, bwd)


def _rope(x, cos, sin, dim):
    reps = x.shape[-1] // LANES
    if reps > 1:
        cos = jnp.concatenate([cos] * reps, axis=1)
        sin = jnp.concatenate([sin] * reps, axis=1)
    return x * cos + _swap_halves(x, dim) * sin


def _pair_q_heads(q):
    b0, b1 = q[:, :LANES], q[:, LANES:]
    lo = _lane_mask(0, GQA_HEAD_DIM)
    r0 = pltpu.roll(b0, GQA_HEAD_DIM, 1)
    r1 = pltpu.roll(b1, GQA_HEAD_DIM, 1)
    return jnp.concatenate([jnp.where(lo, b0, r1), jnp.where(lo, r0, b1)], axis=1)


def _fold8(x, op):
    return op(x.reshape(x.shape[0] // 8, 8, x.shape[1]), axis=0)


class _Unit:
    def __init__(self, k_ref, k_lanes, qt_blk, masks, vt_ref, v_rows, out_row0, is_diff):
        self.k_ref, self.k_lanes, self.vt_ref, self.v_rows = k_ref, k_lanes, vt_ref, v_rows
        self.out_row0, self.is_diff = out_row0, is_diff
        row = lax.broadcasted_iota(jnp.int32, (LANES, 1), 0)
        self.qm = [jnp.where((row >= lo) & (row < hi), qt_blk, jnp.zeros_like(qt_blk))
                   for lo, hi in masks]
        self.m = self.scale = None


def _make_units(qbt, qct, kb_ref, kc_ref, vbt_ref, vct_ref):
    gqa_units, diff_units = [], []
    for pos, head in enumerate(_Q_HEAD_ORDER):
        blk, kv = pos // 2, pos % 2
        rows = slice(kv * GQA_HEAD_DIM, (kv + 1) * GQA_HEAD_DIM)
        gqa_units.append(_Unit(kb_ref, slice(0, LANES), qbt[blk * LANES:(blk + 1) * LANES, :],
                               [(rows.start, rows.stop)], vbt_ref, rows, head * GQA_HEAD_DIM, False))
    for head in range(DIFF_HEADS):
        blk, base = head // 2, (head % 2) * DIFF_V_DIM
        lanes = slice(blk * LANES, (blk + 1) * LANES)
        diff_units.append(_Unit(kc_ref, lanes, qct[lanes, :],
                                [(base, base + DIFF_QK_DIM), (base + DIFF_QK_DIM, base + DIFF_V_DIM)],
                                vct_ref, slice(head * DIFF_V_DIM, (head + 1) * DIFF_V_DIM),
                                BRANCH_W + head * DIFF_V_DIM, True))
    return [u for pair in zip(gqa_units, diff_units) for u in pair]


def _attention_pipeline(units, lam, s_buf, ot_ref):
    n = len(units)
    tq = s_buf.shape[-1]
    for j in range(n + 2):
        ua = units[j] if j < n else None
        ub = units[j - 1] if 1 <= j <= n else None
        uc = units[j - 2] if 2 <= j <= n + 1 and units[j - 2].is_diff else None
        sa, sb, sc = j % 3, (j - 1) % 3, (j - 2) % 3
        init = {}
        if ua is not None:
            init["mx"] = [jnp.full((8, tq), -jnp.inf, F32) for _ in ua.qm]
        if ub is not None:
            init["l"] = [jnp.zeros((8, tq), F32) for _ in ub.qm]
            if not ub.is_diff:
                init["ob"] = jnp.zeros((DIFF_V_DIM, tq), F32)
        if uc is not None:
            init["oc"] = jnp.zeros((DIFF_V_DIM, tq), F32)

        def body(c, carry, ua=ua, ub=ub, uc=uc, sa=sa, sb=sb, sc=sc):
            new = dict(carry)
            if ua is not None:
                k_c = ua.k_ref[c, :, ua.k_lanes]
                mx = []
                for mi, qm in enumerate(ua.qm):
                    s = _dot(k_c, qm)
                    s_buf[sa, mi, c] = s
                    mx.append(jnp.maximum(carry["mx"][mi], _fold8(s, jnp.max)))
                new["mx"] = mx
            if ub is not None:
                ls = []
                for mi in range(len(ub.qm)):
                    e = jnp.exp2(s_buf[sb, mi, c] - ub.m[mi])
                    ls.append(carry["l"][mi] + _fold8(e, jnp.sum))
                    if ub.is_diff:
                        s_buf[sb, mi, c] = e
                    else:
                        new["ob"] = carry["ob"] + _dot(ub.vt_ref[c, ub.v_rows, :], e.astype(BF16))
                new["l"] = ls
            if uc is not None:
                w = s_buf[sc, 0, c] - s_buf[sc, 1, c] * uc.scale[1]
                new["oc"] = carry["oc"] + _dot(uc.vt_ref[c, uc.v_rows, :], w.astype(BF16))
            return new

        out = lax.fori_loop(0, s_buf.shape[2], body, init, unroll=True)
        if ua is not None:
            ua.m = [jnp.max(mx, axis=0, keepdims=True) for mx in out["mx"]]
        if ub is not None:
            l = [jnp.sum(x, axis=0, keepdims=True) for x in out["l"]]
            if ub.is_diff:
                ub.scale = [1.0 / l[0], lam * l[0] / l[1]]
            else:
                ot_ref[ub.out_row0:ub.out_row0 + DIFF_V_DIM, :] = out["ob"] * (1.0 / l[0])
        if uc is not None:
            o = out["oc"] * uc.scale[0]
            ot_ref[uc.out_row0:uc.out_row0 + DIFF_V_DIM, :] = (
                o * lax.rsqrt(jnp.mean(o * o, axis=0, keepdims=True) + EPS))


def _lambda(lamv, layer):
    lam_init = 0.8 - 0.6 * math.exp(-0.3 * layer)
    a = jnp.sum(lamv[0:1] * lamv[1:2], axis=-1, keepdims=True)
    b = jnp.sum(lamv[2:3] * lamv[3:4], axis=-1, keepdims=True)
    return jnp.exp(a) - jnp.exp(b) + lam_init, 1.0 - lam_init


def _sgu(du, dvn, wsp, bias):
    lane = lax.broadcasted_iota(jnp.int32, (1, BRANCH_W), 1)
    outs = []
    for n in range(du.shape[0] // SGU_CHUNK):
        rows = slice(n * SGU_CHUNK, (n + 1) * SGU_CHUNK)
        r = _dot(wsp, dvn[rows].astype(BF16))
        s = bias
        for g in range(SGU_GROUPS):
            gm = (lane >= g * 64) & (lane < (g + 1) * 64)
            s = s + jnp.where(gm, r[g * SGU_CHUNK:(g + 1) * SGU_CHUNK], 0.0)
        outs.append(du[rows] * s)
    return outs[0] if len(outs) == 1 else jnp.concatenate(outs, axis=0)


def _fourier_stage1(a, bdc, bds):
    return jnp.concatenate([_dot(a, bdc), _dot(a, bds)], axis=0).astype(BF16)


def _fourier_stage2(dft_rows, f, length, w_fourier):
    y = _dot(dft_rows, f) * (1.0 / math.sqrt(length))
    return _dot(y.astype(BF16), w_fourier)


def _ada_kernel(cond_ref, w_ref, b_ref, o_ref):
    c = cond_ref[...]
    s = c * jax.nn.sigmoid(c)
    o_ref[...] = _dot(s.astype(BF16), w_ref[...].astype(BF16)) + b_ref[...]


def _ctx_kernel(layer, x_ref, mod_ref, g1_ref, win_ref, gq_ref, gk_ref, gsgu_ref, gdiff_ref,
                lamv_ref, dft_ref, bdc_ref, bds_ref, wf_ref, wsp_ref, bsp_ref,
                br_ref, ok_ref, ov_ref, odk_ref, odv_ref,
                kb_s, vbt_s, kc_s, vct_s, s_buf, ot_s):
    x = x_ref[...]
    mod = mod_ref[...]
    h = _rms_mod(x, g1_ref[...], mod[1:2], mod[0:1])
    proj = _dot(h, win_ref[...])
    qb = proj[:, _QB0:_KB0]
    kb = proj[:, _KB0:_VB0]
    qb = _pair_q_heads(qb * lax.rsqrt(_group_mean_sq(qb, GQA_HEAD_DIM) + EPS) * gq_ref[...])
    kb = kb * lax.rsqrt(_group_mean_sq(kb, GQA_HEAD_DIM) + EPS) * gk_ref[...]
    vb = proj[:, _VB0:_QC0]
    qc = proj[:, _QC0:_KC0]
    kc = proj[:, _KC0:_VC0]
    vc = proj[:, _VC0:_DU0]
    du = _gelu(proj[:, _DU0:_DV0])
    dv = _gelu(proj[:, _DV0:IN_W])
    dvn = dv * lax.rsqrt(jnp.mean(dv * dv, axis=-1, keepdims=True) + EPS) * gsgu_ref[...]
    lam, lam_scale = _lambda(lamv_ref[...], layer)
    qb = qb * _GQA_Q_SCALE
    qc = qc * _DIFF_Q_SCALE
    a = proj[:, _A0:_QB0].astype(BF16)
    for s in range(CTX_SEQS_PER_STEP):
        rows = slice(s * SEQ, (s + 1) * SEQ)
        ok_ref[s] = kb[rows]
        ov_ref[s] = vb[rows]
        odk_ref[s] = kc[rows]
        odv_ref[s] = vc[rows]
        kb_s[s, 0] = kb[rows].astype(BF16)
        kc_s[s, 0] = kc[rows].astype(BF16)
        vbt_s[s, 0] = vb[rows].T.astype(BF16)
        vct_s[s, 0] = vc[rows].T.astype(BF16)
        f = _fourier_stage1(a[rows], bdc_ref[...], bds_ref[...])
        br_ref[rows, 0:256] = _fourier_stage2(dft_ref[...], f, SEQ, wf_ref[...]).astype(BF16)
        br_ref[rows, 768:1024] = _sgu(du[rows], dvn[rows], wsp_ref[...], bsp_ref[...]).astype(BF16)
    for s in range(CTX_SEQS_PER_STEP):
        rows = slice(s * SEQ, (s + 1) * SEQ)
        units = _make_units(qb[rows].T.astype(BF16), qc[rows].T.astype(BF16),
                            kb_s.at[s], kc_s.at[s], vbt_s.at[s], vct_s.at[s])
        _attention_pipeline(units, lam, s_buf.at[s], ot_s.at[s])
        br_ref[rows, 256:512] = ot_s[s, 0:BRANCH_W, :].T.astype(BF16)
        br_ref[rows, 512:768] = (ot_s[s, BRANCH_W:2 * BRANCH_W, :].T
                                 * (gdiff_ref[...] * lam_scale)).astype(BF16)


def _lat_pre_kernel(x_ref, mod_ref, g1_ref, win_ref, gq_ref, gk_ref, gsgu_ref,
                    cb_ref, sb_ref, cc_ref, sc_ref,
                    a_ref, qb_ref, kb_ref, vbt_ref, qc_ref, kc_ref, vct_ref, du_ref, dvn_ref):
    x = x_ref[...]
    mod = mod_ref[...]
    h = _rms_mod(x, g1_ref[...], mod[1:2], mod[0:1])
    cb, sb, cc, sc = cb_ref[...], sb_ref[...], cc_ref[...], sc_ref[...]
    proj = lambda c0, c1: _dot(h, win_ref[:, c0:c1])
    qb = proj(_QB0, _KB0)
    kvb = proj(_KB0, _QC0)
    kb, vb = kvb[:, :LANES], kvb[:, LANES:]
    qb = qb * lax.rsqrt(_group_mean_sq(qb, GQA_HEAD_DIM) + EPS) * gq_ref[...]
    kb = kb * lax.rsqrt(_group_mean_sq(kb, GQA_HEAD_DIM) + EPS) * gk_ref[...]
    qb_ref[...] = _pair_q_heads(_rope(qb, cb, sb, GQA_HEAD_DIM) * _GQA_Q_SCALE).T.astype(BF16)
    kb_ref[...] = _rope(kb, cb, sb, GQA_HEAD_DIM).astype(BF16)
    vbt_ref[...] = vb.T.astype(BF16)
    qc = proj(_QC0, _KC0)
    kc = proj(_KC0, _VC0)
    qc_ref[...] = (_rope(qc, cc, sc, DIFF_QK_DIM) * _DIFF_Q_SCALE).T.astype(BF16)
    kc_ref[...] = _rope(kc, cc, sc, DIFF_QK_DIM).astype(BF16)
    vc = proj(_VC0, _DU0)
    du = proj(_DU0, _DV0)
    vct_ref[...] = vc.T.astype(BF16)
    du_ref[...] = _gelu(du)
    dv = _gelu(proj(_DV0, IN_W))
    a_ref[...] = proj(_A0, _QB0).astype(BF16)
    dvn_ref[...] = (dv * lax.rsqrt(jnp.mean(dv * dv, axis=-1, keepdims=True) + EPS)
                    * gsgu_ref[...]).astype(BF16)


def _lat_mix_kernel(layer, a_ref, qb_ref, kb_ref, vbt_ref, qc_ref, kc_ref, vct_ref,
                    du_ref, dvn_ref, ck_ref, cv_ref, cdk_ref, cdv_ref, gdiff_ref, lamv_ref,
                    dft_ref, bdc_ref, bds_ref, wf_ref, wsp_ref, bsp_ref,
                    br_ref, kb_s, vbt_s, kc_s, vct_s, f_s, s_buf, ot_s):
    kc = KEY_CHUNK
    past_chunks = PAST_LEN // kc

    @pl.when(pl.program_id(1) == 0)
    def _fill():
        for c in range(KEYS_LAT // kc):
            if c < past_chunks:
                rows = slice(c * kc, (c + 1) * kc)
                kb_s[c] = ck_ref[rows, :].astype(BF16)
                kc_s[c] = cdk_ref[rows, :].astype(BF16)
                vbt_s[c] = cv_ref[rows, :].T.astype(BF16)
                vct_s[c] = cdv_ref[rows, :].T.astype(BF16)
            else:
                rows = slice((c - past_chunks) * kc, (c - past_chunks + 1) * kc)
                kb_s[c] = kb_ref[rows, :]
                kc_s[c] = kc_ref[rows, :]
                vbt_s[c] = vbt_ref[:, rows]
                vct_s[c] = vct_ref[:, rows]
        f_s[...] = _fourier_stage1(a_ref[...], bdc_ref[...], bds_ref[...])

    lam, lam_scale = _lambda(lamv_ref[...], layer)
    units = _make_units(qb_ref[...], qc_ref[...], kb_s, kc_s, vbt_s, vct_s)
    _attention_pipeline(units, lam, s_buf, ot_s)
    br_ref[:, 0:256] = _fourier_stage2(dft_ref[...], f_s[...], DEC_SEQ, wf_ref[...]).astype(BF16)
    br_ref[:, 768:1024] = _sgu(du_ref[...], dvn_ref[...].astype(F32), wsp_ref[...],
                               bsp_ref[...]).astype(BF16)
    br_ref[:, 256:512] = ot_s[0:BRANCH_W, :].T.astype(BF16)
    br_ref[:, 512:768] = (ot_s[BRANCH_W:2 * BRANCH_W, :].T * (gdiff_ref[...] * lam_scale)).astype(BF16)


def _cast_slabs(n_cast, refs, step=None, steps=CAST_STEPS):
    step = pl.program_id(0) if step is None else step

    @pl.when(step < steps)
    def _():
        for src, dst in zip(refs[:n_cast], refs[n_cast:]):
            dst[...] = src[...].astype(BF16)


def _merge_kernel(n_cast, xc_ref, xl_ref, brc_ref, brl_ref, mod_ref, g1_ref, wg_ref, wb_ref, wo_ref,
                  *rest):
    cast_src, o_ref, cast_dst = rest[:n_cast], rest[n_cast], rest[n_cast + 1:]
    is_ctx = pl.program_id(0) < CTX_ROWS // ROW_TM
    x = jnp.where(is_ctx, xc_ref[...], xl_ref[...])
    br = jnp.where(is_ctx, brc_ref[...], brl_ref[...])
    mod = mod_ref[...]
    h_half = _rms_mod(x, g1_ref[...], mod[1:2], mod[0:1], out_scale=0.5)
    br_half = br * 0.5
    merged = None
    for n in range(N_BRANCH):
        t = jnp.tanh(_dot(h_half, wg_ref[:, n * D_MODEL:(n + 1) * D_MODEL]))
        term = (t + 1.0) * _dot(br_half[:, n * BRANCH_W:(n + 1) * BRANCH_W],
                                wb_ref[n * BRANCH_W:(n + 1) * BRANCH_W, :])
        merged = term if merged is None else merged + term
    o_ref[...] = x + mod[2:3] * _dot(merged.astype(BF16), wo_ref[...])
    _cast_slabs(n_cast, cast_src + cast_dst)


def _mlp_block(x, mod, g2, w1_ref, w2_ref):
    h = _rms_mod(x, g2, mod[4:5], mod[3:4])
    acc = None
    for n in range(D_FF // D_MODEL):
        cols = slice(n * D_MODEL, (n + 1) * D_MODEL)
        hid = jnp.maximum(_dot(h, w1_ref[:, cols]), 0.0)
        term = _dot((hid * hid).astype(BF16), w2_ref[cols, :])
        acc = term if acc is None else acc + term
    return x + mod[5:6] * acc


def _mlp_kernel(x_ref, mod_ref, g2_ref, w1_ref, w2_ref, cond_ref, wada_ref, bada_ref, o_ref, modn_ref):
    o_ref[...] = _mlp_block(x_ref[...], mod_ref[...], g2_ref[...], w1_ref, w2_ref)
    _ada_kernel(cond_ref, wada_ref, bada_ref, modn_ref)


def _mlp_final_kernel(x_ref, mod_ref, g2_ref, w1_ref, w2_ref, gf_ref, yc_ref, yl_ref):
    x = _mlp_block(x_ref[...], mod_ref[...], g2_ref[...], w1_ref, w2_ref)
    y = x * lax.rsqrt(jnp.mean(x * x, axis=-1, keepdims=True) + EPS) * gf_ref[...]
    is_ctx = pl.program_id(0) < CTX_ROWS // MLP_TM

    @pl.when(is_ctx)
    def _():
        yc_ref[...] = y

    @pl.when(jnp.logical_not(is_ctx))
    def _():
        yl_ref[...] = y


def _params(*sem):
    return pltpu.CompilerParams(dimension_semantics=sem, vmem_limit_bytes=VMEM_LIMIT)


def _full(shape):
    nd = len(shape)
    return pl.BlockSpec(shape, lambda *_: (0,) * nd)


def _layer_block(shape, layer):
    nd = len(shape)
    return pl.BlockSpec((None,) + tuple(shape), lambda *_: (layer,) + (0,) * nd)


def _resident(shape):
    nd = len(shape)
    return pl.BlockSpec(shape, lambda *_: (0,) * nd, pipeline_mode=pl.Buffered(1))


def _cast_job(w, layer, steps=CAST_STEPS, step_of=lambda i: i):
    _, r, c = w.shape
    slab = r // steps
    step = lambda *idx: jnp.minimum(step_of(*idx), steps - 1)
    return (pl.BlockSpec((None, slab, c), lambda *idx: (layer, step(*idx), 0)),
            pl.BlockSpec((slab, c), lambda *idx: (step(*idx), 0)),
            jax.ShapeDtypeStruct((r, c), BF16))


def _mod_spec(select):
    return pl.BlockSpec((None, N_MOD, D_MODEL), lambda *idx: (select(*idx), 0, 0))


def _row_mod_select(tm):
    ctx_tiles = CTX_ROWS // tm
    tiles_per_seq = DEC_SEQ // tm
    return lambda i: jnp.where(i < ctx_tiles, 0, 1 + (i - ctx_tiles) // tiles_per_seq)


def _ada_call(cond, w_ada, b_ada):
    return pl.pallas_call(
        _ada_kernel,
        grid=(N_MOD,),
        in_specs=[
            pl.BlockSpec((COND_ROWS, D_MODEL), lambda j: (0, 0)),
            pl.BlockSpec((None, D_MODEL, D_MODEL), lambda j: (0, 0, j)),
            pl.BlockSpec((None, 1, D_MODEL), lambda j: (0, 0, j)),
        ],
        out_specs=pl.BlockSpec((COND_ROWS, D_MODEL), lambda j: (0, j)),
        out_shape=jax.ShapeDtypeStruct((COND_ROWS, N_MOD * D_MODEL), F32),
        compiler_params=_params("arbitrary"),
        name="ada",
    )(cond, w_ada, b_ada)


def _ctx_call(layer, x, mods, p, w, caches, casts=()):
    ns = CTX_SEQS_PER_STEP
    tm = ns * SEQ
    cache_shapes = [(BATCH, DEPTH, SEQ, 128), (BATCH, DEPTH, SEQ, 128),
                    (BATCH, DEPTH, SEQ, 256), (BATCH, DEPTH, SEQ, 256)]
    in_specs = [
        pl.BlockSpec((tm, D_MODEL), lambda i: (i, 0)),
        _mod_spec(lambda i: 0),
        _layer_block((1, D_MODEL), layer),
        _resident((D_MODEL, IN_W)),
        _layer_block((1, 256), layer),
        _layer_block((1, 128), layer),
        _layer_block((1, 256), layer),
        _layer_block((1, 256), layer),
        _layer_block((4, DIFF_QK_DIM), layer),
        _full((SEQ, 2 * SEQ)),
        _full((BRANCH_W, BRANCH_W)),
        _full((BRANCH_W, BRANCH_W)),
        _layer_block((BRANCH_W, BRANCH_W), layer),
        _layer_block((SGU_GROUPS * SGU_CHUNK, SGU_CHUNK), layer),
        _layer_block((SGU_CHUNK, BRANCH_W), layer),
    ]
    args = [x, mods, p["norm1_g"], w["w_in"], p["gq"], p["gk"], p["gsgu"], p["gdiff"], p["lamv"],
            p["dft_ctx"], p["bdc"], p["bds"], p["w_fourier"], p["wsp"], p["bsp"]]
    n_base = len(args)
    steps = BATCH // ns
    jobs = [_cast_job(c, layer, steps) for c in casts]
    in_specs += [j[0] for j in jobs]
    args += list(casts)
    aliases = {}
    n_alias = 0
    if caches is not None:
        n_alias = len(caches)
        for j, cache in enumerate(caches):
            in_specs.append(pl.BlockSpec(memory_space=pl.ANY))
            aliases[len(args)] = 1 + j
            args.append(cache)
    n_cast, n_out = len(jobs), 1 + len(cache_shapes)

    def kernel(*refs):
        base_in, cast_src = refs[:n_base], refs[n_base:n_base + n_cast]
        rest = refs[n_base + n_cast + n_alias:]
        outs, cast_dst, scratch = rest[:n_out], rest[n_out:n_out + n_cast], rest[n_out + n_cast:]
        _ctx_kernel(layer, *base_in, *outs, *scratch)
        if n_cast:
            _cast_slabs(n_cast, cast_src + cast_dst, steps=steps)

    out_specs = [pl.BlockSpec((tm, D_MODEL), lambda i: (i, 0))] + [
        pl.BlockSpec((ns, None, SEQ, s[-1]), lambda i: (i, layer, 0, 0)) for s in cache_shapes]
    out_shape = [jax.ShapeDtypeStruct((CTX_ROWS, D_MODEL), BF16)] + [
        jax.ShapeDtypeStruct(s, F32) for s in cache_shapes]
    out_specs += [j[1] for j in jobs]
    out_shape += [j[2] for j in jobs]
    outs = pl.pallas_call(
        kernel,
        grid=(steps,),
        in_specs=in_specs,
        out_specs=out_specs,
        out_shape=out_shape,
        scratch_shapes=[
            pltpu.VMEM((ns, 1, SEQ, 128), BF16), pltpu.VMEM((ns, 1, 128, SEQ), BF16),
            pltpu.VMEM((ns, 1, SEQ, 256), BF16), pltpu.VMEM((ns, 1, 256, SEQ), BF16),
            pltpu.VMEM((ns, 3, 2, 1, SEQ, SEQ), F32),
            pltpu.VMEM((ns, 2 * BRANCH_W, SEQ), F32),
        ],
        input_output_aliases=aliases,
        compiler_params=_params("arbitrary"),
        name=f"ctx_mix_{layer}",
    )(*args)
    return outs[0], tuple(outs[1:n_out]), list(outs[n_out:])


def _lat_pre_call(layer, x, row0, mods, p, w):
    tm = LAT_PRE_TM
    tile0 = row0 // tm
    tiles_per_seq = DEC_SEQ // tm
    rope_spec = pl.BlockSpec((tm, LANES), lambda i: (i % tiles_per_seq, 0))
    rows = lambda w, dt: (pl.BlockSpec((tm, w), lambda i: (i, 0)),
                          jax.ShapeDtypeStruct((LAT_ROWS, w), dt))
    cols = lambda w: (pl.BlockSpec((None, w, tm), lambda i: (i // tiles_per_seq, 0, i % tiles_per_seq)),
                      jax.ShapeDtypeStruct((DEC_BATCH, w, DEC_SEQ), BF16))
    outs = [rows(256, BF16), cols(256), rows(128, BF16), cols(128),
            cols(256), rows(256, BF16), cols(256), rows(256, F32), rows(256, BF16)]
    return pl.pallas_call(
        _lat_pre_kernel,
        grid=(LAT_ROWS // tm,),
        in_specs=[
            pl.BlockSpec((tm, D_MODEL), lambda i: (tile0 + i, 0)),
            _mod_spec(lambda i: 1 + i // tiles_per_seq),
            _layer_block((1, D_MODEL), layer),
            _resident((D_MODEL, IN_W)),
            _layer_block((1, 256), layer),
            _layer_block((1, 128), layer),
            _layer_block((1, 256), layer),
            rope_spec, rope_spec, rope_spec, rope_spec,
        ],
        out_specs=[spec for spec, _ in outs],
        out_shape=[shape for _, shape in outs],
        compiler_params=_params("arbitrary"),
        name=f"lat_pre_{layer}",
    )(x, mods, p["norm1_g"], w["w_in"], p["gq"], p["gk"], p["gsgu"],
      p["rope_cb"], p["rope_sb"], p["rope_cc"], p["rope_sc"])


def _lat_mix_call(layer, pre, caches, p, casts=()):
    tq = LAT_TQ
    nq = DEC_SEQ // tq
    n_chunks = KEYS_LAT // KEY_CHUNK
    steps = DEC_BATCH * nq
    linear = lambda b, q: b * nq + q
    jobs = [_cast_job(c, layer, steps, linear) for c in casts]
    n_in, n_cast = 21, len(jobs)

    def kernel(*refs):
        base_in, cast_src = refs[:n_in], refs[n_in:n_in + n_cast]
        br_ref, cast_dst = refs[n_in + n_cast], refs[n_in + n_cast + 1:n_in + 2 * n_cast + 1]
        _lat_mix_kernel(layer, *base_in, br_ref, *refs[n_in + 2 * n_cast + 1:])
        if n_cast:
            _cast_slabs(n_cast, cast_src + cast_dst, step=linear(pl.program_id(0), pl.program_id(1)),
                        steps=steps)

    seq_block = lambda w: pl.BlockSpec((DEC_SEQ, w), lambda b, q: (b, 0))
    seq_block_t = lambda w: pl.BlockSpec((None, w, DEC_SEQ), lambda b, q: (b, 0, 0))
    q_block = lambda w: pl.BlockSpec((tq, w), lambda b, q: (b * nq + q, 0))
    q_block_t = lambda w: pl.BlockSpec((None, w, tq), lambda b, q: (b, 0, q))
    cache_block = lambda w: pl.BlockSpec((None, None, PAST_LEN, w), lambda b, q: (b, layer, 0, 0))
    outs = pl.pallas_call(
        kernel,
        grid=(DEC_BATCH, nq),
        in_specs=[
            seq_block(256), q_block_t(256), seq_block(128), seq_block_t(128),
            q_block_t(256), seq_block(256), seq_block_t(256),
            q_block(256), q_block(256),
            cache_block(128), cache_block(128), cache_block(256), cache_block(256),
            _layer_block((1, 256), layer),
            _layer_block((4, DIFF_QK_DIM), layer),
            pl.BlockSpec((tq, 2 * DEC_SEQ), lambda b, q: (q, 0)),
            _full((BRANCH_W, BRANCH_W)),
            _full((BRANCH_W, BRANCH_W)),
            _layer_block((BRANCH_W, BRANCH_W), layer),
            _layer_block((SGU_GROUPS * SGU_CHUNK, SGU_CHUNK), layer),
            _layer_block((SGU_CHUNK, BRANCH_W), layer),
        ] + [j[0] for j in jobs],
        out_specs=[q_block(D_MODEL)] + [j[1] for j in jobs],
        out_shape=[jax.ShapeDtypeStruct((LAT_ROWS, D_MODEL), BF16)] + [j[2] for j in jobs],
        scratch_shapes=[
            pltpu.VMEM((n_chunks, KEY_CHUNK, 128), BF16), pltpu.VMEM((n_chunks, 128, KEY_CHUNK), BF16),
            pltpu.VMEM((n_chunks, KEY_CHUNK, 256), BF16), pltpu.VMEM((n_chunks, 256, KEY_CHUNK), BF16),
            pltpu.VMEM((2 * DEC_SEQ, BRANCH_W), BF16),
            pltpu.VMEM((3, 2, n_chunks, KEY_CHUNK, tq), F32),
            pltpu.VMEM((2 * BRANCH_W, tq), F32),
        ],
        compiler_params=_params("arbitrary", "arbitrary"),
        name=f"lat_mix_{layer}",
    )(*pre, *caches, p["gdiff"], p["lamv"],
      p["dft_lat"], p["bdc"], p["bds"], p["w_fourier"], p["wsp"], p["bsp"], *casts)
    return outs[0], list(outs[1:])


def _merge_call(layer, x_ctx, x_lat, lat_row0, br_ctx, br_lat, mods, p, w, casts):
    tm = ROW_TM
    ctx_tiles = CTX_ROWS // tm
    lat_tile0 = lat_row0 // tm
    ctx_tile = lambda i: (jnp.minimum(i, ctx_tiles - 1), 0)
    jobs = [_cast_job(c, layer + 1) for c in casts]
    outs = pl.pallas_call(
        functools.partial(_merge_kernel, len(jobs)),
        grid=(ALL_ROWS // tm,),
        in_specs=[
            pl.BlockSpec((tm, D_MODEL), ctx_tile),
            pl.BlockSpec((tm, D_MODEL), lambda i: (lat_tile0 + jnp.maximum(i - ctx_tiles, 0), 0)),
            pl.BlockSpec((tm, D_MODEL), ctx_tile),
            pl.BlockSpec((tm, D_MODEL), lambda i: (jnp.maximum(i - ctx_tiles, 0), 0)),
            _mod_spec(_row_mod_select(tm)),
            _layer_block((1, D_MODEL), layer),
            _resident((D_MODEL, N_BRANCH * D_MODEL)),
            _resident((N_BRANCH * BRANCH_W, D_MODEL)),
            _resident((D_MODEL, D_MODEL)),
        ] + [j[0] for j in jobs],
        out_specs=[pl.BlockSpec((tm, D_MODEL), lambda i: (i, 0))] + [j[1] for j in jobs],
        out_shape=[jax.ShapeDtypeStruct((ALL_ROWS, D_MODEL), F32)] + [j[2] for j in jobs],
        compiler_params=_params("arbitrary"),
        name=f"merge_{layer}",
    )(x_ctx, x_lat, br_ctx, br_lat, mods, p["norm1_g"], w["w_gate"], w["w_branch"], w["w_out"], *casts)
    return outs[0], outs[1:]


def _mlp_call(layer, x, mods, p, w, ada=None, final_g=None):
    tm = MLP_TM
    steps = ALL_ROWS // tm
    ctx_tiles = CTX_ROWS // tm
    in_specs = [
        pl.BlockSpec((tm, D_MODEL), lambda i: (i, 0)),
        _mod_spec(_row_mod_select(tm)),
        _layer_block((1, D_MODEL), layer),
        _resident((D_MODEL, D_FF)),
        _resident((D_FF, D_MODEL)),
    ]
    args = [x, mods, p["norm2_g"], w["w_mlp1"], w["w_mlp2"]]
    if final_g is None:
        ada_w = N_MOD * D_MODEL // steps
        body = _mlp_kernel
        in_specs += [
            _full((COND_ROWS, D_MODEL)),
            pl.BlockSpec((None, D_MODEL, ada_w), lambda i: (layer + 1, 0, i)),
            pl.BlockSpec((None, 1, ada_w), lambda i: (layer + 1, 0, i)),
        ]
        args += list(ada)
        out_specs = [pl.BlockSpec((tm, D_MODEL), lambda i: (i, 0)),
                     pl.BlockSpec((COND_ROWS, ada_w), lambda i: (0, i))]
        out_shape = [jax.ShapeDtypeStruct((ALL_ROWS, D_MODEL), F32),
                     jax.ShapeDtypeStruct((COND_ROWS, N_MOD * D_MODEL), F32)]
    else:
        body = _mlp_final_kernel
        in_specs.append(_full((1, D_MODEL)))
        args.append(final_g)
        out_specs = [
            pl.BlockSpec((tm, D_MODEL), lambda i: (jnp.minimum(i, ctx_tiles - 1), 0)),
            pl.BlockSpec((tm, D_MODEL), lambda i: (jnp.maximum(i - ctx_tiles, 0), 0)),
        ]
        out_shape = [jax.ShapeDtypeStruct((CTX_ROWS, D_MODEL), F32),
                     jax.ShapeDtypeStruct((LAT_ROWS, D_MODEL), F32)]
    return pl.pallas_call(
        body,
        grid=(steps,),
        in_specs=in_specs,
        out_specs=out_specs,
        out_shape=out_shape,
        compiler_params=_params("arbitrary"),
        name=f"mlp_{layer}",
    )(*args)


def kernel(x_prompt, x_sample, c, cache_gqa_k, cache_gqa_v, cache_diff_k, cache_diff_v, c_ctx, w_ada, b_ada, norm1_g, norm2_g, w_in, w_fourier, q_norm_g, k_norm_g, lambda_q1, lambda_k1, lambda_q2, lambda_k2, diff_norm_g, sgu_norm_g, w_spatial, b_spatial, w_gate, w_branch, w_out, w_mlp1, w_mlp2, final_norm_g):
    bdc, bds = _channel_dft_tables()
    cb, sb = _rope_tables(DEC_SEQ, GQA_HEAD_DIM)
    cc, sc = _rope_tables(DEC_SEQ, DIFF_QK_DIM)
    p = {
        "norm1_g": norm1_g.reshape(DEPTH, 1, D_MODEL),
        "norm2_g": norm2_g.reshape(DEPTH, 1, D_MODEL),
        "w_fourier": w_fourier.astype(BF16),
        "gq": jnp.tile(q_norm_g, (1, GQA_HEADS)).reshape(DEPTH, 1, 256),
        "gk": jnp.tile(k_norm_g, (1, GQA_KV_HEADS)).reshape(DEPTH, 1, 128),
        "gsgu": sgu_norm_g.reshape(DEPTH, 1, BRANCH_W),
        "gdiff": jnp.tile(diff_norm_g, (1, DIFF_HEADS)).reshape(DEPTH, 1, 256),
        "lamv": jnp.stack([lambda_q1, lambda_k1, lambda_q2, lambda_k2], axis=1),
        "wsp": w_spatial.reshape(DEPTH, SGU_GROUPS * SGU_CHUNK, SGU_CHUNK).astype(BF16),
        "bsp": jnp.repeat(jnp.swapaxes(b_spatial, 1, 2), BRANCH_W // SGU_GROUPS, axis=2),
        "dft_ctx": jnp.asarray(_dft_tables(SEQ)).astype(BF16),
        "dft_lat": jnp.asarray(_dft_tables(DEC_SEQ)).astype(BF16),
        "bdc": jnp.asarray(bdc).astype(BF16),
        "bds": jnp.asarray(bds).astype(BF16),
        "rope_cb": jnp.asarray(cb), "rope_sb": jnp.asarray(sb),
        "rope_cc": jnp.asarray(cc), "rope_sc": jnp.asarray(sc),
    }
    lat_caches = (
        cache_gqa_k.reshape(DEC_BATCH, DEPTH, PAST_LEN, 128),
        cache_gqa_v.reshape(DEC_BATCH, DEPTH, PAST_LEN, 128),
        cache_diff_k.reshape(DEC_BATCH, DEPTH, PAST_LEN, 256),
        cache_diff_v.reshape(DEC_BATCH, DEPTH, PAST_LEN, 256),
    )

    cond = jnp.concatenate(
        [c_ctx[None, :], c, jnp.zeros((COND_ROWS - 1 - DEC_BATCH, D_MODEL), F32)], axis=0)
    ada = (cond, w_ada, b_ada.reshape(DEPTH, 1, N_MOD * D_MODEL))
    mods = _ada_call(*ada).reshape(COND_ROWS, N_MOD, D_MODEL)

    big = {"w_in": w_in, "w_gate": w_gate,
           "w_branch": w_branch.reshape(DEPTH, N_BRANCH * BRANCH_W, D_MODEL), "w_out": w_out,
           "w_mlp1": w_mlp1, "w_mlp2": w_mlp2}
    w = {"w_in": w_in[0].astype(BF16)}
    ctx_casts = {k: big[k] for k in ("w_gate", "w_branch", "w_out")}
    lat_casts = {k: big[k] for k in ("w_mlp1", "w_mlp2")}

    x_ctx, x_lat, lat_row0 = x_prompt.reshape(CTX_ROWS, D_MODEL), x_sample.reshape(LAT_ROWS, D_MODEL), 0
    new_caches = None
    for layer in range(DEPTH):
        last = layer + 1 == DEPTH
        first = layer == 0
        br_ctx, new_caches, cast_c = _ctx_call(layer, x_ctx, mods, p, w, new_caches,
                                               list(ctx_casts.values()) if first else [])
        pre = _lat_pre_call(layer, x_lat, lat_row0, mods, p, w)
        br_lat, cast_l = _lat_mix_call(layer, pre, lat_caches, p,
                                       list(lat_casts.values()) if first else [])
        if first:
            w.update(zip(list(ctx_casts) + list(lat_casts), cast_c + cast_l))
        x, next_w = _merge_call(layer, x_ctx, x_lat, lat_row0, br_ctx, br_lat, mods, p, w,
                                [] if last else list(big.values()))
        if last:
            y_ctx, y_lat = _mlp_call(layer, x, mods, p, w, final_g=final_norm_g.reshape(1, D_MODEL))
        else:
            x, mods = _mlp_call(layer, x, mods, p, w, ada=ada)
            mods = mods.reshape(COND_ROWS, N_MOD, D_MODEL)
            w = dict(zip(big, next_w))
            x_ctx, x_lat, lat_row0 = x, x, CTX_ROWS

    gk, gv, dk, dv = new_caches
    return (
        y_ctx.reshape(BATCH, SEQ, D_MODEL),
        y_lat.reshape(DEC_BATCH, DEC_SEQ, D_MODEL),
        gk.reshape(BATCH, DEPTH, SEQ, GQA_KV_HEADS, GQA_HEAD_DIM),
        gv.reshape(BATCH, DEPTH, SEQ, GQA_KV_HEADS, GQA_HEAD_DIM),
        dk.reshape(BATCH, DEPTH, SEQ, DIFF_HEADS, 2, DIFF_QK_DIM),
        dv.reshape(BATCH, DEPTH, SEQ, DIFF_HEADS, DIFF_V_DIM),
    )
```
